```python
import math
import numpy as np
import jax
import jax.numpy as jnp
from jax import lax

D_MODEL = 4096
BATCH = 8
SEQ = 2048
DEPTH = 4

GRID_W = 64
CTX_LEN = 256
N_MIXERS = 3
LAYER_KIND = tuple(i % N_MIXERS for i in range(DEPTH))
LAYER_SLOT = tuple(LAYER_KIND[:i].count(LAYER_KIND[i]) for i in range(DEPTH))
N_SWA = LAYER_KIND.count(0)
N_FNET = LAYER_KIND.count(1)
N_NAT = LAYER_KIND.count(2)

HEAD_DIM = 128
N_HEADS = D_MODEL // HEAD_DIM
N_KV_HEADS = 8
GQA_GROUP = N_HEADS // N_KV_HEADS
WINDOW = 128
ATTN_BLOCK = 128
ROPE_THETA = 10000.0
ROPE_AXIS_DIM = HEAD_DIM // 2

NAT_HEADS = D_MODEL // HEAD_DIM
NAT_KH = 8
NAT_KW = 16
NAT_QC = 16
NAT_GW = NAT_QC + NAT_KW

FOURIER_GROUPS = 8
FOURIER_GROUP_DIM = D_MODEL // FOURIER_GROUPS

N_EXPERTS = 32
TOP_K = 4
N_EXPERT_GROUPS = 4
TOPK_GROUPS = 2
EXPERT_HIDDEN = 256
SHARED_HIDDEN = 1024
ROUTED_SCALE = 2.5

ADA_RANK = 256
N_MOD = 6
EPS = 1e-6

kernel_name = "hybrid_swa_fnet_nat_moe_dit"

F32 = jnp.float32


def rms_norm(x, g):
    xf = x.astype(F32)
    y = xf * lax.rsqrt(jnp.mean(xf * xf, axis=-1, keepdims=True) + EPS)
    return (y * g.astype(F32)).astype(x.dtype)


def modulate(h, shift, scale):
    return h * (1 + scale) + shift


def ada_mod(cond, w_down, w_up, b):
    m = (jax.nn.silu(cond) @ w_down) @ w_up + b
    return jnp.split(m, N_MOD, axis=-1)


def axial_rope_tables(n_tokens):
    pos = jnp.arange(n_tokens)
    row = (pos // GRID_W).astype(F32)
    col = (pos % GRID_W).astype(F32)
    inv = ROPE_THETA ** (-jnp.arange(0, ROPE_AXIS_DIM, 2, dtype=F32) / ROPE_AXIS_DIM)
    ar = row[:, None, None] * inv
    ac = col[:, None, None] * inv
    return (jnp.cos(ar), jnp.sin(ar), jnp.cos(ac), jnp.sin(ac))


def _rotate(x, cos, sin):
    x1, x2 = jnp.split(x, 2, axis=-1)
    return jnp.concatenate([x1 * cos - x2 * sin, x2 * cos + x1 * sin], axis=-1)


def apply_axial_rope(x, rope):
    cr, sr, cc, sc = rope
    xr, xc = jnp.split(x.astype(F32), 2, axis=-1)
    return jnp.concatenate([_rotate(xr, cr, sr), _rotate(xc, cc, sc)], axis=-1).astype(x.dtype)


def ctx_self_attention(q, k, v, sink):
    s = jnp.einsum('bqkgd,bjkd->bkgqj', q, k).astype(F32) * (HEAD_DIM ** -0.5)
    if sink is not None:
        s = jnp.concatenate([s, jnp.broadcast_to(sink[None, :, :, None, None], s.shape[:-1] + (1,))], axis=-1)
    p = jax.nn.softmax(s, axis=-1)[..., :k.shape[1]].astype(v.dtype)
    return jnp.einsum('bkgqj,bjkd->bqkgd', p, v)


def swa_mixer(hx, hc, w_qkv, w_o, q_gain, k_gain, sink, rope, ctx_out):
    B, S, _ = hx.shape
    L = hc.shape[1]
    nq = N_HEADS * HEAD_DIM
    nkv = N_KV_HEADS * HEAD_DIM
    qkv = hx @ w_qkv
    q = rms_norm(qkv[..., :nq].reshape(B, S, N_HEADS, HEAD_DIM), q_gain)
    q = apply_axial_rope(q, rope).reshape(B, S, N_KV_HEADS, GQA_GROUP, HEAD_DIM)
    k = apply_axial_rope(rms_norm(qkv[..., nq:nq + nkv].reshape(B, S, N_KV_HEADS, HEAD_DIM), k_gain), rope)
    v = qkv[..., nq + nkv:].reshape(B, S, N_KV_HEADS, HEAD_DIM)
    if ctx_out:
        qkv_c = hc @ w_qkv
        q_c = rms_norm(qkv_c[..., :nq].reshape(B, L, N_HEADS, HEAD_DIM), q_gain)
        q_c = q_c.reshape(B, L, N_KV_HEADS, GQA_GROUP, HEAD_DIM)
        kv_c = qkv_c[..., nq:]
    else:
        kv_c = hc @ w_qkv[:, nq:]
    k_c = rms_norm(kv_c[..., :nkv].reshape(B, L, N_KV_HEADS, HEAD_DIM), k_gain)
    v_c = kv_c[..., nkv:].reshape(B, L, N_KV_HEADS, HEAD_DIM)

    scale = HEAD_DIM ** -0.5
    sink_g = sink.astype(F32).reshape(N_KV_HEADS, GQA_GROUP)
    span = ATTN_BLOCK + 2 * WINDOW
    pad = ((0, 0), (WINDOW, WINDOW), (0, 0), (0, 0))
    kp = jnp.pad(k, pad)
    vp = jnp.pad(v, pad)

    def block(b):
        q0 = b * ATTN_BLOCK
        qb = lax.dynamic_slice_in_dim(q, q0, ATTN_BLOCK, axis=1)
        kb = lax.dynamic_slice_in_dim(kp, q0, span, axis=1)
        vb = lax.dynamic_slice_in_dim(vp, q0, span, axis=1)
        qpos = q0 + jnp.arange(ATTN_BLOCK)
        kpos = q0 - WINDOW + jnp.arange(span)
        valid = (jnp.abs(qpos[:, None] - kpos[None, :]) <= WINDOW) & (kpos >= 0) & (kpos < S)
        s_loc = jnp.einsum('bqkgd,bjkd->bkgqj', qb, kb).astype(F32) * scale
        s_loc = jnp.where(valid, s_loc, -jnp.inf)
        s_ctx = jnp.einsum('bqkgd,bjkd->bkgqj', qb, k_c).astype(F32) * scale
        s_snk = jnp.broadcast_to(sink_g[None, :, :, None, None], s_loc.shape[:-1] + (1,))
        p = jax.nn.softmax(jnp.concatenate([s_loc, s_ctx, s_snk], axis=-1), axis=-1)
        p_loc = p[..., :span].astype(vb.dtype)
        p_ctx = p[..., span:span + L].astype(vb.dtype)
        return (jnp.einsum('bkgqj,bjkd->bqkgd', p_loc, vb)
                + jnp.einsum('bkgqj,bjkd->bqkgd', p_ctx, v_c))

    o = lax.map(block, jnp.arange(S // ATTN_BLOCK))
    ox = jnp.moveaxis(o, 0, 1).reshape(B, S, nq) @ w_o
    oc = None
    if ctx_out:
        oc = ctx_self_attention(q_c, k_c, v_c, sink_g).reshape(B, L, nq) @ w_o
    return ox, oc


def fourier_2d_real(h):
    B, N, D = h.shape
    hf = h.astype(F32).reshape(B, N, FOURIER_GROUPS, FOURIER_GROUP_DIM)
    y = jnp.fft.fft2(hf, axes=(1, 3), norm="ortho").real
    return y.reshape(B, N, D).astype(h.dtype)


def fnet_mixer(hx, hc, w_o, ctx_out):
    ox = fourier_2d_real(hx) @ w_o
    oc = fourier_2d_real(hc) @ w_o if ctx_out else None
    return ox, oc


def nat_mixer(hx, hc, w_qkv, w_o, q_gain, k_gain, rpb, ctx_out):
    B, S, _ = hx.shape
    L = hc.shape[1]
    rows = S // GRID_W
    kh = min(NAT_KH, rows)
    width = NAT_HEADS * HEAD_DIM
    scale = HEAD_DIM ** -0.5
    qkv = hx @ w_qkv
    q, k, v = jnp.split(qkv, 3, axis=-1)
    grid = (B, rows, GRID_W, NAT_HEADS, HEAD_DIM)
    q = rms_norm(q.reshape(grid), q_gain)
    k = rms_norm(k.reshape(grid), k_gain)
    v = v.reshape(grid)
    if ctx_out:
        qkv_c = hc @ w_qkv
        q_c = rms_norm(qkv_c[..., :width].reshape(B, L, NAT_HEADS, HEAD_DIM), q_gain)
        kv_c = qkv_c[..., width:]
    else:
        kv_c = hc @ w_qkv[:, width:]
    k_c = rms_norm(kv_c[..., :width].reshape(B, L, NAT_HEADS, HEAD_DIM), k_gain)
    v_c = kv_c[..., width:].reshape(B, L, NAT_HEADS, HEAD_DIM)

    n_cb = GRID_W // NAT_QC
    qcol = np.arange(GRID_W).reshape(n_cb, NAT_QC)
    g0 = np.clip(np.arange(n_cb) * NAT_QC - NAT_KW // 2, 0, GRID_W - NAT_GW)
    gcols = g0[:, None] + np.arange(NAT_GW)
    c0 = np.clip(qcol - NAT_KW // 2, 0, GRID_W - NAT_KW)
    col_valid = (gcols[:, None, :] >= c0[..., None]) & (gcols[:, None, :] < c0[..., None] + NAT_KW)
    dcol = np.clip(gcols[:, None, :] - qcol[..., None] + NAT_KW - 1, 0, 2 * NAT_KW - 2)
    bias_col = jnp.take(rpb.astype(F32), dcol, axis=2)
    mask = np.broadcast_to(col_valid[:, :, None, :], (n_cb, NAT_QC, kh, NAT_GW)).reshape(n_cb, NAT_QC, kh * NAT_GW)

    def row(r):
        r0 = jnp.clip(r - kh // 2, 0, rows - kh)
        qr = lax.dynamic_index_in_dim(q, r, axis=1, keepdims=False).reshape(B, n_cb, NAT_QC, NAT_HEADS, HEAD_DIM)
        kr = jnp.take(lax.dynamic_slice_in_dim(k, r0, kh, axis=1), gcols, axis=2)
        vr = jnp.take(lax.dynamic_slice_in_dim(v, r0, kh, axis=1), gcols, axis=2)
        kr = jnp.moveaxis(kr, 1, 2).reshape(B, n_cb, kh * NAT_GW, NAT_HEADS, HEAD_DIM)
        vr = jnp.moveaxis(vr, 1, 2).reshape(B, n_cb, kh * NAT_GW, NAT_HEADS, HEAD_DIM)
        drow = r0 + jnp.arange(kh) - r + NAT_KH - 1
        bias = jnp.take(bias_col, drow, axis=1).transpose(0, 2, 3, 1, 4).reshape(NAT_HEADS, n_cb, NAT_QC, kh * NAT_GW)
        s_loc = jnp.einsum('bjqhd,bjkhd->bhjqk', qr, kr).astype(F32) * scale + bias
        s_loc = jnp.where(mask, s_loc, -jnp.inf)
        s_ctx = jnp.einsum('bjqhd,blhd->bhjql', qr, k_c).astype(F32) * scale
        p = jax.nn.softmax(jnp.concatenate([s_loc, s_ctx], axis=-1), axis=-1)
        nk = kh * NAT_GW
        o = (jnp.einsum('bhjqk,bjkhd->bjqhd', p[..., :nk].astype(vr.dtype), vr)
             + jnp.einsum('bhjql,blhd->bjqhd', p[..., nk:].astype(vr.dtype), v_c))
        return o.reshape(B, GRID_W, width)

    o = lax.map(row, jnp.arange(rows))
    ox = jnp.moveaxis(o, 0, 1).reshape(B, S, width) @ w_o
    oc = None
    if ctx_out:
        oc = ctx_self_attention(q_c.reshape(B, L, NAT_HEADS, 1, HEAD_DIM), k_c, v_c, None).reshape(B, L, width) @ w_o
    return ox, oc


def moe_ffn(h, w_router, e_bias, w_gate, w_up, w_down, s_gate, s_up, s_down):
    B, N, D = h.shape
    t = h.reshape(B * N, D)
    scores = jax.nn.sigmoid((t @ w_router).astype(F32))
    sel = scores + e_bias.astype(F32)
    grp = sel.reshape(-1, N_EXPERT_GROUPS, N_EXPERTS // N_EXPERT_GROUPS)
    grp_score = lax.top_k(grp, 2)[0].sum(-1)
    _, grp_idx = lax.top_k(grp_score, TOPK_GROUPS)
    grp_mask = jax.nn.one_hot(grp_idx, N_EXPERT_GROUPS, dtype=F32).sum(-2)
    exp_mask = jnp.repeat(grp_mask, N_EXPERTS // N_EXPERT_GROUPS, axis=-1) > 0
    _, idx = lax.top_k(jnp.where(exp_mask, sel, -jnp.inf), TOP_K)
    wts = jnp.take_along_axis(scores, idx, axis=-1)
    wts = wts / jnp.sum(wts, axis=-1, keepdims=True) * ROUTED_SCALE
    gates = jnp.einsum('tk,tke->te', wts, jax.nn.one_hot(idx, N_EXPERTS, dtype=F32)).astype(h.dtype)
    hid = jax.nn.silu(jnp.einsum('td,edf->tef', t, w_gate)) * jnp.einsum('td,edf->tef', t, w_up)
    routed = jnp.einsum('tef,efd->td', hid * gates[..., None], w_down)
    shared = (jax.nn.silu(t @ s_gate) * (t @ s_up)) @ s_down
    return (routed + shared).reshape(B, N, D)


def setup_inputs(seed: int = 0) -> dict:
    key = jax.random.key(seed)
    ks = iter(jax.random.split(key, 40))

    def nrm(shape, std):
        return jax.random.normal(next(ks), shape, F32) * std

    def gain(shape):
        return 1.0 + nrm(shape, 0.02)

    D = D_MODEL
    return {
        "x": nrm((BATCH, SEQ, D), 1.0),
        "c": nrm((BATCH, D), 1.0),
        "ctx": nrm((BATCH, CTX_LEN, D), 1.0),
        "c_ctx": nrm((D,), 1.0),
        "ada_down": nrm((DEPTH, D, ADA_RANK), D ** -0.5),
        "ada_up": nrm((DEPTH, ADA_RANK, N_MOD * D), 0.3 * ADA_RANK ** -0.5),
        "ada_bias": nrm((DEPTH, N_MOD * D), 0.02),
        "norm1_g": gain((DEPTH, D)),
        "norm2_g": gain((DEPTH, D)),
        "swa_w_qkv": nrm((N_SWA, D, (N_HEADS + 2 * N_KV_HEADS) * HEAD_DIM), D ** -0.5),
        "swa_w_o": nrm((N_SWA, N_HEADS * HEAD_DIM, D), (N_HEADS * HEAD_DIM) ** -0.5),
        "swa_q_gain": gain((N_SWA, HEAD_DIM)),
        "swa_k_gain": gain((N_SWA, HEAD_DIM)),
        "swa_sink": nrm((N_SWA, N_HEADS), 1.0),
        "fnet_w_o": nrm((N_FNET, D, D), D ** -0.5),
        "nat_w_qkv": nrm((N_NAT, D, 3 * NAT_HEADS * HEAD_DIM), D ** -0.5),
        "nat_w_o": nrm((N_NAT, NAT_HEADS * HEAD_DIM, D), (NAT_HEADS * HEAD_DIM) ** -0.5),
        "nat_q_gain": gain((N_NAT, HEAD_DIM)),
        "nat_k_gain": gain((N_NAT, HEAD_DIM)),
        "nat_rpb": nrm((N_NAT, NAT_HEADS, 2 * NAT_KH - 1, 2 * NAT_KW - 1), 0.5),
        "moe_router": nrm((DEPTH, D, N_EXPERTS), D ** -0.5),
        "moe_bias": nrm((DEPTH, N_EXPERTS), 0.01),
        "moe_w_gate": nrm((DEPTH, N_EXPERTS, D, EXPERT_HIDDEN), D ** -0.5),
        "moe_w_up": nrm((DEPTH, N_EXPERTS, D, EXPERT_HIDDEN), D ** -0.5),
        "moe_w_down": nrm((DEPTH, N_EXPERTS, EXPERT_HIDDEN, D), EXPERT_HIDDEN ** -0.5),
        "shared_w_gate": nrm((DEPTH, D, SHARED_HIDDEN), D ** -0.5),
        "shared_w_up": nrm((DEPTH, D, SHARED_HIDDEN), D ** -0.5),
        "shared_w_down": nrm((DEPTH, SHARED_HIDDEN, D), SHARED_HIDDEN ** -0.5),
    }


def reference(x, c, ctx, c_ctx, ada_down, ada_up, ada_bias, norm1_g, norm2_g,
              swa_w_qkv, swa_w_o, swa_q_gain, swa_k_gain, swa_sink,
              fnet_w_o,
              nat_w_qkv, nat_w_o, nat_q_gain, nat_k_gain, nat_rpb,
              moe_router, moe_bias, moe_w_gate, moe_w_up, moe_w_down,
              shared_w_gate, shared_w_up, shared_w_down):
    S = x.shape[1]
    rope = axial_rope_tables(S)
    h, hc = x, ctx
    for layer in range(DEPTH):
        kind, slot = LAYER_KIND[layer], LAYER_SLOT[layer]
        ctx_out = layer < DEPTH - 1
        mx = [m[:, None, :] for m in ada_mod(c, ada_down[layer], ada_up[layer], ada_bias[layer])]
        mc = ada_mod(c_ctx, ada_down[layer], ada_up[layer], ada_bias[layer])
        xn = modulate(rms_norm(h, norm1_g[layer]), mx[0], mx[1])
        cn = modulate(rms_norm(hc, norm1_g[layer]), mc[0], mc[1])
        if kind == 0:
            ox, oc = swa_mixer(xn, cn, swa_w_qkv[slot], swa_w_o[slot], swa_q_gain[slot], swa_k_gain[slot],
                               swa_sink[slot], rope, ctx_out)
        elif kind == 1:
            ox, oc = fnet_mixer(xn, cn, fnet_w_o[slot], ctx_out)
        else:
            ox, oc = nat_mixer(xn, cn, nat_w_qkv[slot], nat_w_o[slot], nat_q_gain[slot], nat_k_gain[slot],
                               nat_rpb[slot], ctx_out)
        moe_args = (moe_router[layer], moe_bias[layer], moe_w_gate[layer], moe_w_up[layer], moe_w_down[layer],
                    shared_w_gate[layer], shared_w_up[layer], shared_w_down[layer])
        h = h + mx[2] * ox
        h = h + mx[5] * moe_ffn(modulate(rms_norm(h, norm2_g[layer]), mx[3], mx[4]), *moe_args)
        if ctx_out:
            hc = hc + mc[2] * oc
            hc = hc + mc[5] * moe_ffn(modulate(rms_norm(hc, norm2_g[layer]), mc[3], mc[4]), *moe_args)
    return h
```

```python
import functools

import numpy as np
import jax
import jax.numpy as jnp
from jax import lax
from jax.experimental import pallas as pl
from jax.experimental.pallas import tpu as pltpu

F32 = jnp.float32
BF16 = jnp.bfloat16

D_MODEL = 4096
DEPTH = 4
GRID_W = 64
HEAD_DIM = 128
N_HEADS = 32
N_KV_HEADS = 8
GQA_GROUP = N_HEADS // N_KV_HEADS
WINDOW = 128
ROPE_THETA = 10000.0
ROPE_AXIS_DIM = HEAD_DIM // 2
NAT_KH = 8
NAT_KW = 16
NAT_ROWS_PER_BLOCK = 4
NAT_BLOCK = NAT_ROWS_PER_BLOCK * GRID_W
FOURIER_GROUPS = 8
FOURIER_GROUP_DIM = D_MODEL // FOURIER_GROUPS
N_EXPERTS = 32
TOP_K = 4
N_EXPERT_GROUPS = 4
GROUP_SIZE = N_EXPERTS // N_EXPERT_GROUPS
TOPK_GROUPS = 2
EXPERT_HIDDEN = 256
SHARED_HIDDEN = 1024
ROUTED_SCALE = 2.5
ADA_RANK = 256
N_MOD = 6
EPS = 1e-6
MASKED = -1e30

VMEM_LIMIT_BYTES = 52 * 1024 * 1024
ROW_TILE = 1024
COL_TILE = 512
EXPERT_TILE = 256
COND_ROWS = 16


def _params(*sem):
    return pltpu.CompilerParams(dimension_semantics=sem, vmem_limit_bytes=VMEM_LIMIT_BYTES)


def _dot(a, b):
    return jnp.dot(a, b, preferred_element_type=F32)


def _dot_nt(a, b):
    return lax.dot_general(a, b, (((1,), (1,)), ((), ())), preferred_element_type=F32)


def _silu(x):
    return x * (1.0 / (1.0 + jnp.exp(-x)))


def _ada_kernel(cond_ref, down_ref, up_ref, bias_ref, o_ref):
    c = cond_ref[...]
    z = _dot(_silu(c).astype(BF16), down_ref[0].astype(BF16))
    o_ref[0] = _dot(z.astype(BF16), up_ref[0].astype(BF16)) + bias_ref[0]


def _ada_mods(cond, ada_down, ada_up, ada_bias):
    return pl.pallas_call(
        _ada_kernel,
        grid=(DEPTH, N_MOD),
        in_specs=[
            pl.BlockSpec((COND_ROWS, D_MODEL), lambda l, j: (0, 0)),
            pl.BlockSpec((1, D_MODEL, ADA_RANK), lambda l, j: (l, 0, 0)),
            pl.BlockSpec((1, ADA_RANK, D_MODEL), lambda l, j: (l, 0, j)),
            pl.BlockSpec((1, 1, D_MODEL), lambda l, j: (l, 0, j)),
        ],
        out_specs=pl.BlockSpec((1, COND_ROWS, D_MODEL), lambda l, j: (l, 0, j)),
        out_shape=jax.ShapeDtypeStruct((DEPTH, COND_ROWS, N_MOD * D_MODEL), F32),
        compiler_params=_params("arbitrary", "arbitrary"),
        name="ada_mods",
    )(cond, ada_down, ada_up, ada_bias[:, None, :])


def _norm_mod_tile(h_ref, g_ref, shift_ref, scale_ref):
    x = h_ref[...]
    ms = jnp.mean(x * x, axis=-1, keepdims=True)
    y = x * lax.rsqrt(ms + EPS) * g_ref[...]
    return y * (1.0 + scale_ref[0, 0]) + shift_ref[0, 0]


def _norm_mod_kernel(h_ref, g_ref, shift_ref, scale_ref, o_ref):
    o_ref[...] = _norm_mod_tile(h_ref, g_ref, shift_ref, scale_ref).astype(BF16)


def _split_bf16(x):
    hi = x.astype(BF16)
    lo = (x - hi.astype(F32)).astype(BF16)
    return hi, lo


def _first_index_of_max(v, idx, n):
    m = jnp.max(v, axis=0, keepdims=True)
    i = jnp.min(jnp.where(v == m, idx, n), axis=0, keepdims=True)
    return m, i


def _norm_route_kernel(h_ref, g_ref, shift_ref, scale_ref, wr_ref, eb_ref, o_ref, idx_ref, wts_ref):
    xn = _norm_mod_tile(h_ref, g_ref, shift_ref, scale_ref)
    o_ref[...] = xn.astype(BF16)
    x_hi, x_lo = _split_bf16(xn)
    w_hi, w_lo = _split_bf16(wr_ref[...])
    logits = _dot(x_hi, w_hi) + _dot(x_lo, w_hi) + _dot(x_hi, w_lo)
    lt = logits.T[:N_EXPERTS]
    scores = 1.0 / (1.0 + jnp.exp(-lt))
    sel = scores + eb_ref[...]
    n_tok = sel.shape[1]
    sub = lax.broadcasted_iota(jnp.int32, (GROUP_SIZE, n_tok), 0).astype(F32)
    grp_scores = []
    for g in range(N_EXPERT_GROUPS):
        sg = sel[g * GROUP_SIZE:(g + 1) * GROUP_SIZE]
        m1, i1 = _first_index_of_max(sg, sub, float(GROUP_SIZE))
        m2 = jnp.max(jnp.where(sub == i1, -jnp.inf, sg), axis=0, keepdims=True)
        grp_scores.append(m1 + m2)
    eidx_i = lax.broadcasted_iota(jnp.int32, sel.shape, 0)
    eidx = eidx_i.astype(F32)
    egrp = (eidx_i // GROUP_SIZE).astype(F32)
    best = jnp.full((1, n_tok), -jnp.inf, F32)
    for g in range(N_EXPERT_GROUPS):
        best = jnp.maximum(best, grp_scores[g])
    g1 = jnp.full((1, n_tok), float(N_EXPERT_GROUPS), F32)
    for g in reversed(range(N_EXPERT_GROUPS)):
        g1 = jnp.where(grp_scores[g] == best, float(g), g1)
    second = jnp.full((1, n_tok), -jnp.inf, F32)
    for g in range(N_EXPERT_GROUPS):
        second = jnp.maximum(second, jnp.where(g1 == float(g), -jnp.inf, grp_scores[g]))
    g2 = jnp.full((1, n_tok), float(N_EXPERT_GROUPS), F32)
    for g in reversed(range(N_EXPERT_GROUPS)):
        g2 = jnp.where((grp_scores[g] == second) & (g1 != float(g)), float(g), g2)
    masked = jnp.where((egrp == g1) | (egrp == g2), sel, -jnp.inf)
    picks, weights = [], []
    for _ in range(TOP_K):
        _, ik = _first_index_of_max(masked, eidx, float(N_EXPERTS))
        hit = eidx == ik
        weights.append(jnp.sum(jnp.where(hit, scores, 0.0), axis=0, keepdims=True))
        picks.append(ik)
        masked = jnp.where(hit, -jnp.inf, masked)
    w = jnp.concatenate(weights, axis=0)
    idx_ref[...] = jnp.concatenate(picks, axis=0).astype(jnp.int32)
    wts_ref[...] = w / jnp.sum(w, axis=0, keepdims=True) * ROUTED_SCALE


def _seg_of_tile(i, tm, seq, n_batch):
    return jnp.minimum((i * tm) // seq, n_batch)


def _norm_mod(h, g, mods, shift_k, scale_k, rows, seq, n_batch, router=None):
    tm = 256
    seg = functools.partial(_seg_of_tile, tm=tm, seq=seq, n_batch=n_batch)
    in_specs = [
        pl.BlockSpec((tm, D_MODEL), lambda i: (i, 0)),
        pl.BlockSpec((1, D_MODEL), lambda i: (0, 0)),
        pl.BlockSpec((1, 1, 1, D_MODEL), lambda i: (seg(i), shift_k, 0, 0)),
        pl.BlockSpec((1, 1, 1, D_MODEL), lambda i: (seg(i), scale_k, 0, 0)),
    ]
    xn_spec = pl.BlockSpec((tm, D_MODEL), lambda i: (i, 0))
    xn_shape = jax.ShapeDtypeStruct((rows, D_MODEL), BF16)
    if router is None:
        return pl.pallas_call(
            _norm_mod_kernel, grid=(rows // tm,), in_specs=in_specs, out_specs=xn_spec, out_shape=xn_shape,
            compiler_params=_params("arbitrary"), name="norm_mod",
        )(h, g.reshape(1, D_MODEL), mods, mods)
    w_router, e_bias = router
    w_pad = jnp.pad(w_router, ((0, 0), (0, 128 - N_EXPERTS)))
    in_specs += [
        pl.BlockSpec((D_MODEL, 128), lambda i: (0, 0)),
        pl.BlockSpec((N_EXPERTS, 1), lambda i: (0, 0)),
    ]
    return pl.pallas_call(
        _norm_route_kernel, grid=(rows // tm,), in_specs=in_specs,
        out_specs=[xn_spec, pl.BlockSpec((TOP_K, tm), lambda i: (0, i)), pl.BlockSpec((TOP_K, tm), lambda i: (0, i))],
        out_shape=[xn_shape, jax.ShapeDtypeStruct((TOP_K, rows), jnp.int32), jax.ShapeDtypeStruct((TOP_K, rows), F32)],
        compiler_params=_params("arbitrary"), name="norm_route",
    )(h, g.reshape(1, D_MODEL), mods, mods, w_pad, e_bias.reshape(N_EXPERTS, 1))


def _qkv_kernel(a_ref, w_ref, gain_ref, cos_ref, sin_ref, o_ref, *, n_qk_tiles, rope):
    j = pl.program_id(1)
    acc = _dot(a_ref[...], w_ref[...])
    n_heads = acc.shape[1] // HEAD_DIM

    @pl.when(j < n_qk_tiles)
    def _():
        lane = lax.broadcasted_iota(jnp.int32, (acc.shape[0], HEAD_DIM), 1)
        first_half = (lane % (ROPE_AXIS_DIM)) < (ROPE_AXIS_DIM // 2)
        outs = []
        for hh in range(n_heads):
            x = acc[:, hh * HEAD_DIM:(hh + 1) * HEAD_DIM]
            ms = jnp.mean(x * x, axis=-1, keepdims=True)
            y = x * lax.rsqrt(ms + EPS) * gain_ref[:, hh * HEAD_DIM:(hh + 1) * HEAD_DIM]
            if rope:
                quarter = ROPE_AXIS_DIM // 2
                partner = jnp.where(first_half, pltpu.roll(y, HEAD_DIM - quarter, 1), pltpu.roll(y, quarter, 1))
                y = y * cos_ref[...] + partner * sin_ref[...]
            outs.append(y.astype(BF16))
        o_ref[...] = jnp.concatenate(outs, axis=1)

    @pl.when(j >= n_qk_tiles)
    def _():
        o_ref[...] = acc.astype(BF16)


def _qkv_proj(xn, w, gain_row, cos, sin, n_qk_cols, rope):
    rows, width = xn.shape[0], w.shape[1]
    tm, tn = ROW_TILE, COL_TILE
    return pl.pallas_call(
        functools.partial(_qkv_kernel, n_qk_tiles=n_qk_cols // tn, rope=rope),
        grid=(rows // tm, width // tn),
        in_specs=[
            pl.BlockSpec((tm, D_MODEL), lambda i, j: (i, 0)),
            pl.BlockSpec((D_MODEL, tn), lambda i, j: (0, j)),
            pl.BlockSpec((1, tn), lambda i, j: (0, j)),
            pl.BlockSpec((tm, HEAD_DIM), lambda i, j: (i, 0)),
            pl.BlockSpec((tm, HEAD_DIM), lambda i, j: (i, 0)),
        ],
        out_specs=pl.BlockSpec((tm, tn), lambda i, j: (i, j)),
        out_shape=jax.ShapeDtypeStruct((rows, width), BF16),
        compiler_params=_params("arbitrary", "arbitrary"),
        name="qkv_proj",
    )(xn, w, gain_row, cos, sin)


def _mm_res_kernel(a_ref, w_ref, h_ref, gate_ref, o_ref):
    o_ref[...] = h_ref[...] + gate_ref[0, 0] * _dot(a_ref[...], w_ref[...])


def _mm_residual(a, w, h, mods, gate_k, rows, seq, n_batch):
    k_dim, width = w.shape
    tm, tn = ROW_TILE, COL_TILE
    seg = functools.partial(_seg_of_tile, tm=tm, seq=seq, n_batch=n_batch)
    return pl.pallas_call(
        _mm_res_kernel,
        grid=(rows // tm, width // tn),
        in_specs=[
            pl.BlockSpec((tm, k_dim), lambda i, j: (i, 0)),
            pl.BlockSpec((k_dim, tn), lambda i, j: (0, j)),
            pl.BlockSpec((tm, tn), lambda i, j: (i, j)),
            pl.BlockSpec((1, 1, 1, tn), lambda i, j: (seg(i), gate_k, 0, j)),
        ],
        out_specs=pl.BlockSpec((tm, tn), lambda i, j: (i, j)),
        out_shape=jax.ShapeDtypeStruct((rows, width), F32),
        compiler_params=_params("arbitrary", "arbitrary"),
        name="mm_residual",
    )(a, w, h, mods)


def _swiglu_kernel(a_ref, wg_ref, wu_ref, o_ref):
    a = a_ref[...]
    o_ref[...] = (_silu(_dot(a, wg_ref[...])) * _dot(a, wu_ref[...])).astype(BF16)


def _swiglu(a, wg, wu, rows):
    width = wg.shape[1]
    tm, tn = ROW_TILE, COL_TILE
    return pl.pallas_call(
        _swiglu_kernel,
        grid=(rows // tm, width // tn),
        in_specs=[
            pl.BlockSpec((tm, D_MODEL), lambda i, j: (i, 0)),
            pl.BlockSpec((D_MODEL, tn), lambda i, j: (0, j)),
            pl.BlockSpec((D_MODEL, tn), lambda i, j: (0, j)),
        ],
        out_specs=pl.BlockSpec((tm, tn), lambda i, j: (i, j)),
        out_shape=jax.ShapeDtypeStruct((rows, width), BF16),
        compiler_params=_params("arbitrary", "arbitrary"),
        name="shared_swiglu",
    )(a, wg, wu)


def _attn_kernel(*refs, n_local, group, block, kind, has_sink, n_blocks):
    refs = list(refs)
    sink_ref = refs.pop(0) if has_sink else None
    q_ref = refs.pop(0)
    k_loc = [refs.pop(0) for _ in range(n_local)]
    v_loc = [refs.pop(0) for _ in range(n_local)]
    kc_ref, vc_ref = refs.pop(0), refs.pop(0)
    bias_ref = refs.pop(0) if kind == "nat" else None
    o_ref = refs.pop(0)
    head = pl.program_id(1)
    i = pl.program_id(2)
    scale = HEAD_DIM ** -0.5

    q = q_ref[...]
    if group > 1:
        q = jnp.concatenate([q[:, g * HEAD_DIM:(g + 1) * HEAD_DIM] for g in range(group)], axis=0)
    n_rows = q.shape[0]
    s_ctx = _dot_nt(q, kc_ref[...]) * scale
    m = jnp.max(s_ctx, axis=-1, keepdims=True)
    if n_local:
        k = jnp.concatenate([r[...] for r in k_loc], axis=0)
        v = jnp.concatenate([r[...] for r in v_loc], axis=0)
        s_loc = _dot_nt(q, k) * scale
        row = lax.broadcasted_iota(jnp.int32, (n_rows, 1), 0)
        col = lax.broadcasted_iota(jnp.int32, (1, n_local * block), 1)
        if kind == "swa":
            rel = col - block - (row % block)
            lo = jnp.where(i == 0, block, 0)
            hi = jnp.where(i == n_blocks - 1, 2 * block, 3 * block)
            valid = (rel >= -WINDOW) & (rel <= WINDOW) & (col >= lo) & (col < hi)
        else:
            q_row = NAT_ROWS_PER_BLOCK * i + row // GRID_W
            k_row = NAT_ROWS_PER_BLOCK * (i - 1) + col // GRID_W
            r0 = jnp.clip(q_row - NAT_KH // 2, 0, n_blocks * NAT_ROWS_PER_BLOCK - NAT_KH)
            valid = (k_row >= r0) & (k_row < r0 + NAT_KH)
            s_loc = s_loc + bias_ref[0]
        s_loc = jnp.where(valid, s_loc, MASKED)
        m = jnp.maximum(m, jnp.max(s_loc, axis=-1, keepdims=True))
    if has_sink:
        g_of_row = lax.broadcasted_iota(jnp.int32, (n_rows, 1), 0) // (n_rows // group)
        sink = jnp.zeros((n_rows, 1), F32)
        for g in range(group):
            sink = jnp.where(g_of_row == g, sink_ref[head * group + g], sink)
        m = jnp.maximum(m, sink)
    p_ctx = jnp.exp(s_ctx - m)
    denom = jnp.sum(p_ctx, axis=-1, keepdims=True)
    o = _dot(p_ctx.astype(BF16), vc_ref[...])
    if n_local:
        p_loc = jnp.exp(s_loc - m)
        denom = denom + jnp.sum(p_loc, axis=-1, keepdims=True)
        o = o + _dot(p_loc.astype(BF16), v)
    if has_sink:
        denom = denom + jnp.exp(sink - m)
    o = o * (1.0 / denom)
    if group > 1:
        rows_per = n_rows // group
        o = jnp.concatenate([o[g * rows_per:(g + 1) * rows_per] for g in range(group)], axis=1)
    o_ref[...] = o.astype(BF16)


def _attention(qkv, *, kind, latent, n_batch, seq, ctx_len, k_col0, v_col0, group, sink=None, bias=None):
    n_kv = (v_col0 - k_col0)
    block = (WINDOW if kind == "swa" else NAT_BLOCK) if latent else ctx_len
    n_blocks = seq // block if latent else 1
    n_local = 3 if latent else 0
    ctx_blk0 = n_batch * seq // ctx_len
    q_blk0 = 0 if latent else n_batch * seq // block
    per_batch = n_blocks
    has_sink = sink is not None
    nat_order = kind == "nat"

    def ids(a, b, c):
        return (b, a, c) if nat_order else (a, b, c)

    def q_map(a, b, c, *_):
        bb, hh, ii = ids(a, b, c)
        return (q_blk0 + bb * per_batch + ii, hh)

    def loc_map(delta, col0):
        def f(a, b, c, *_):
            bb, hh, ii = ids(a, b, c)
            return (bb * per_batch + jnp.clip(ii + delta, 0, n_blocks - 1), col0 + hh)
        return f

    def ctx_map(col0):
        def f(a, b, c, *_):
            bb, hh, _ii = ids(a, b, c)
            return (ctx_blk0 + bb, col0 + hh)
        return f

    in_specs, args = [], []
    if has_sink:
        in_specs.append(pl.BlockSpec(memory_space=pltpu.SMEM))
        args.append(sink.astype(F32))
    in_specs.append(pl.BlockSpec((block, group * HEAD_DIM), q_map))
    args.append(qkv)
    for col0 in ((k_col0, v_col0) if latent else ()):
        for delta in (-1, 0, 1):
            in_specs.append(pl.BlockSpec((block, HEAD_DIM), loc_map(delta, col0)))
            args.append(qkv)
    for col0 in (k_col0, v_col0):
        in_specs.append(pl.BlockSpec((ctx_len, HEAD_DIM), ctx_map(col0)))
        args.append(qkv)
    if nat_order and latent:
        in_specs.append(pl.BlockSpec((1, block, 3 * block), lambda a, b, c: (a, 0, 0)))
        args.append(bias)
    kernel_kind = kind if latent else "ctx"
    grid = (n_kv, n_batch, n_blocks) if nat_order else (n_batch, n_kv, n_blocks)
    out_rows = n_batch * (seq if latent else ctx_len)

    def o_map(a, b, c, *_):
        bb, hh, ii = ids(a, b, c)
        return (bb * per_batch + ii, hh)

    kern = functools.partial(_attn_kernel, n_local=n_local, group=group, block=block,
                             kind="nat" if nat_order else kernel_kind, has_sink=has_sink, n_blocks=n_blocks)
    if nat_order and not latent:
        kern = functools.partial(_attn_kernel, n_local=0, group=group, block=block, kind="nat_ctx",
                                 has_sink=False, n_blocks=1)
    return pl.pallas_call(
        kern, grid=grid, in_specs=in_specs,
        out_specs=pl.BlockSpec((block, group * HEAD_DIM), o_map),
        out_shape=jax.ShapeDtypeStruct((out_rows, n_kv * group * HEAD_DIM), BF16),
        compiler_params=_params("arbitrary", "arbitrary", "arbitrary"),
        name=f"attn_{kind}_{'latent' if latent else 'ctx'}",
    )(*args)


def _nat_bias_table(rpb):
    q = np.arange(NAT_BLOCK)
    k = np.arange(3 * NAT_BLOCK)
    q_r, q_c = q // GRID_W, q % GRID_W
    k_r, k_c = k // GRID_W, k % GRID_W
    d_row = k_r[None, :] - NAT_ROWS_PER_BLOCK - q_r[:, None] + NAT_KH - 1
    d_col = np.clip(k_c[None, :] - q_c[:, None] + NAT_KW - 1, 0, 2 * NAT_KW - 2)
    c0 = np.clip(q_c - NAT_KW // 2, 0, GRID_W - NAT_KW)
    col_valid = (k_c[None, :] >= c0[:, None]) & (k_c[None, :] < c0[:, None] + NAT_KW)
    bias = rpb.astype(F32)[:, d_row, d_col]
    return jnp.where(col_valid[None], bias, MASKED)


def _dft_chan_kernel(a_ref, w_ref, o_ref):
    o_ref[0] = _dot(a_ref[...], w_ref[0]).astype(BF16)


def _dft_pos_kernel(cn_ref, sn_ref, xc_ref, xs_ref, *rest, scale):
    o_ref = rest[-1]
    o_ref[...] = ((_dot(cn_ref[...], xc_ref[0]) - _dot(sn_ref[...], xs_ref[0])) * scale).astype(BF16)


def _dft_tables(n):
    k = jnp.arange(n, dtype=jnp.int32)
    ang = ((k[:, None] * k[None, :]) % n).astype(F32) * (2.0 * np.pi / n)
    return jnp.cos(ang).astype(BF16), jnp.sin(ang).astype(BF16)


def _fourier_mix(xn, rows, n_batch, seq, ctx_len, ctx_out):
    gd = FOURIER_GROUP_DIM
    cd, sd = _dft_tables(gd)
    chan = jnp.stack([cd, sd])
    tm = ROW_TILE
    xcs = pl.pallas_call(
        _dft_chan_kernel,
        grid=(rows // tm, FOURIER_GROUPS, 2),
        in_specs=[pl.BlockSpec((tm, gd), lambda i, g, t: (i, g)),
                  pl.BlockSpec((1, gd, gd), lambda i, g, t: (t, 0, 0))],
        out_specs=pl.BlockSpec((1, tm, gd), lambda i, g, t: (t, i, g)),
        out_shape=jax.ShapeDtypeStruct((2, rows, D_MODEL), BF16),
        compiler_params=_params("arbitrary", "arbitrary", "arbitrary"),
        name="dft_channels",
    )(xn, chan)

    def pos_call(n, tmo, row_blk0, in_blk0, out_rows, prev):
        cn, sn = _dft_tables(n)
        tn = COL_TILE
        per = n // tmo
        in_specs = [
            pl.BlockSpec((tmo, n), lambda b, i, j: (i, 0)),
            pl.BlockSpec((tmo, n), lambda b, i, j: (i, 0)),
            pl.BlockSpec((1, n, tn), lambda b, i, j: (0, in_blk0 + b, j)),
            pl.BlockSpec((1, n, tn), lambda b, i, j: (1, in_blk0 + b, j)),
        ]
        args = [cn, sn, xcs, xcs]
        aliases = {}
        if prev is not None:
            in_specs.append(pl.BlockSpec(memory_space=pl.ANY))
            args.append(prev)
            aliases = {4: 0}
        return pl.pallas_call(
            functools.partial(_dft_pos_kernel, scale=float((n * gd) ** -0.5)),
            grid=(n_batch, per, D_MODEL // tn),
            in_specs=in_specs,
            out_specs=pl.BlockSpec((tmo, tn), lambda b, i, j: (row_blk0 + b * per + i, j)),
            out_shape=jax.ShapeDtypeStruct((out_rows, D_MODEL), BF16),
            input_output_aliases=aliases,
            compiler_params=_params("arbitrary", "arbitrary", "arbitrary"),
            name=f"dft_positions_{n}",
        )(*args)

    z = pos_call(seq, 512, 0, 0, rows, None)
    if ctx_out:
        z = pos_call(ctx_len, ctx_len, n_batch * seq // ctx_len, n_batch * seq // ctx_len, rows, z)
    return z


def _expert_kernel(te_ref, nu_ref, x_ref, wg_ref, wu_ref, wd_ref, o_ref):
    @pl.when(pl.program_id(0) < nu_ref[0])
    def _():
        x = x_ref[...]
        hid = (_silu(_dot(x, wg_ref[0])) * _dot(x, wu_ref[0])).astype(BF16)
        o_ref[...] = _dot(hid, wd_ref[0]).astype(BF16)


def _experts(x_sorted, tile_expert, n_used, wg, wu, wd):
    n_rows = x_sorted.shape[0]
    tm = EXPERT_TILE

    def row_map(j, te, nu):
        return (jnp.minimum(j, nu[0] - 1), 0)

    def w_map(j, te, nu):
        return (te[j], 0, 0)

    grid_spec = pltpu.PrefetchScalarGridSpec(
        num_scalar_prefetch=2,
        grid=(n_rows // tm,),
        in_specs=[
            pl.BlockSpec((tm, D_MODEL), row_map),
            pl.BlockSpec((1, D_MODEL, EXPERT_HIDDEN), w_map),
            pl.BlockSpec((1, D_MODEL, EXPERT_HIDDEN), w_map),
            pl.BlockSpec((1, EXPERT_HIDDEN, D_MODEL), w_map),
        ],
        out_specs=pl.BlockSpec((tm, D_MODEL), row_map),
    )
    return pl.pallas_call(
        _expert_kernel, grid_spec=grid_spec,
        out_shape=jax.ShapeDtypeStruct((n_rows, D_MODEL), BF16),
        compiler_params=_params("arbitrary"),
        name="routed_experts",
    )(tile_expert, n_used, x_sorted, wg, wu, wd)


def _combine_kernel(h_ref, y_ref, w_ref, gate_ref, o_ref):
    w = w_ref[...]
    acc = w[:, 0:1] * y_ref[0].astype(F32)
    for k in range(1, TOP_K):
        acc = acc + w[:, k:k + 1] * y_ref[k].astype(F32)
    o_ref[...] = h_ref[...] + gate_ref[0, 0] * acc


def _combine(h, y_tok, wts_t, mods, gate_k, rows, seq, n_batch):
    tm = 256
    seg = functools.partial(_seg_of_tile, tm=tm, seq=seq, n_batch=n_batch)
    return pl.pallas_call(
        _combine_kernel,
        grid=(rows // tm,),
        in_specs=[
            pl.BlockSpec((tm, D_MODEL), lambda i: (i, 0)),
            pl.BlockSpec((TOP_K, tm, D_MODEL), lambda i: (0, i, 0)),
            pl.BlockSpec((tm, TOP_K), lambda i: (i, 0)),
            pl.BlockSpec((1, 1, 1, D_MODEL), lambda i: (seg(i), gate_k, 0, 0)),
        ],
        out_specs=pl.BlockSpec((tm, D_MODEL), lambda i: (i, 0)),
        out_shape=jax.ShapeDtypeStruct((rows, D_MODEL), F32),
        compiler_params=_params("arbitrary"),
        name="moe_combine",
    )(h, y_tok, wts_t, mods)


def _dispatch_plan(idx, rows):
    tm = EXPERT_TILE
    flat = idx.reshape(-1)
    onehot = (flat[:, None] == jnp.arange(N_EXPERTS, dtype=jnp.int32)[None, :]).astype(jnp.int32)
    csum = jnp.cumsum(onehot, axis=0)
    rank = jnp.sum(csum * onehot, axis=1) - 1
    counts = csum[-1]
    padded = (counts + tm - 1) // tm * tm
    ends = jnp.cumsum(padded)
    offs = ends - padded
    pos = jnp.sum(onehot * offs[None, :], axis=1) + rank
    n_slots = TOP_K * rows + N_EXPERTS * tm
    n_used = ends[-1] // tm
    tile_start = jnp.arange(n_slots // tm, dtype=jnp.int32) * tm
    tile_expert = jnp.sum((tile_start[:, None] >= ends[None, :]).astype(jnp.int32), axis=1)
    last_expert = jnp.sum((((n_used - 1) * tm) >= ends).astype(jnp.int32))
    tile_expert = jnp.where(tile_start < ends[-1], tile_expert, last_expert).astype(jnp.int32)
    token = jnp.arange(TOP_K * rows, dtype=jnp.int32) % rows
    slot_token = jnp.zeros((n_slots,), jnp.int32).at[pos].set(token)
    return pos, slot_token, tile_expert, n_used.astype(jnp.int32).reshape(1)


def _moe(h, g2, mods, rows, seq, n_batch, w_router, e_bias, wg, wu, wd, sg, su, sd):
    xn, idx, wts = _norm_mod(h, g2, mods, 3, 4, rows, seq, n_batch, router=(w_router, e_bias))
    hid_s = _swiglu(xn, sg.astype(BF16), su.astype(BF16), rows)
    h = _mm_residual(hid_s, sd.astype(BF16), h, mods, 5, rows, seq, n_batch)
    pos, slot_token, tile_expert, n_used = _dispatch_plan(idx, rows)
    x_sorted = jnp.take(xn, slot_token, axis=0)
    y_sorted = _experts(x_sorted, tile_expert, n_used, wg.astype(BF16), wu.astype(BF16), wd.astype(BF16))
    y_tok = jnp.take(y_sorted, pos, axis=0).reshape(TOP_K, rows, D_MODEL)
    return _combine(h, y_tok, wts.T, mods, 5, rows, seq, n_batch)


def _rope_tables(n_batch, seq, total_rows):
    pos = jnp.arange(seq)
    row = (pos // GRID_W).astype(F32)
    col = (pos % GRID_W).astype(F32)
    inv = ROPE_THETA ** (-jnp.arange(0, ROPE_AXIS_DIM, 2, dtype=F32) / ROPE_AXIS_DIM)
    ar = row[:, None] * inv
    ac = col[:, None] * inv
    cos = jnp.concatenate([jnp.cos(ar), jnp.cos(ar), jnp.cos(ac), jnp.cos(ac)], axis=-1)
    sin = jnp.concatenate([-jnp.sin(ar), jnp.sin(ar), -jnp.sin(ac), jnp.sin(ac)], axis=-1)
    n_ctx_rows = total_rows - n_batch * seq
    cos = jnp.concatenate([jnp.tile(cos, (n_batch, 1)), jnp.ones((n_ctx_rows, HEAD_DIM), F32)], axis=0)
    sin = jnp.concatenate([jnp.tile(sin, (n_batch, 1)), jnp.zeros((n_ctx_rows, HEAD_DIM), F32)], axis=0)
    return cos, sin


def kernel(x, c, ctx, c_ctx, ada_down, ada_up, ada_bias, norm1_g, norm2_g, swa_w_qkv, swa_w_o, swa_q_gain, swa_k_gain, swa_sink, fnet_w_o, nat_w_qkv, nat_w_o, nat_q_gain, nat_k_gain, nat_rpb, moe_router, moe_bias, moe_w_gate, moe_w_up, moe_w_down, shared_w_gate, shared_w_up, shared_w_down):
    n_batch, seq, _ = x.shape
    ctx_len = ctx.shape[1]
    lat_rows = n_batch * seq
    all_rows = lat_rows + n_batch * ctx_len
    assert n_batch + 1 <= COND_ROWS and seq % ROW_TILE == 0 and lat_rows % ROW_TILE == 0 and all_rows % ROW_TILE == 0
    assert seq // GRID_W >= NAT_KH and seq % NAT_BLOCK == 0 and ctx_len == NAT_BLOCK

    h = jnp.concatenate([x.reshape(lat_rows, D_MODEL), ctx.reshape(n_batch * ctx_len, D_MODEL)], axis=0)
    cond = jnp.concatenate([c, c_ctx[None, :], jnp.zeros((COND_ROWS - n_batch - 1, D_MODEL), F32)], axis=0)
    mods_all = _ada_mods(cond, ada_down, ada_up, ada_bias).reshape(DEPTH, COND_ROWS, N_MOD, 1, D_MODEL)
    cos, sin = _rope_tables(n_batch, seq, all_rows)
    ones_row = jnp.ones((N_HEADS * HEAD_DIM,), F32)

    slot = [0, 0, 0]
    for layer in range(DEPTH):
        kind = layer % 3
        s = slot[kind]
        slot[kind] += 1
        ctx_out = layer < DEPTH - 1
        rows = all_rows if ctx_out else lat_rows
        mods = mods_all[layer]
        common = dict(n_batch=n_batch, seq=seq, ctx_len=ctx_len)
        xn = _norm_mod(h, norm1_g[layer], mods, 0, 1, all_rows, seq, n_batch)
        if kind == 0:
            nq, nkv = N_HEADS * HEAD_DIM, N_KV_HEADS * HEAD_DIM
            gain = jnp.concatenate([jnp.tile(swa_q_gain[s], N_HEADS), jnp.tile(swa_k_gain[s], N_KV_HEADS),
                                    ones_row[:nkv]])[None, :]
            qkv = _qkv_proj(xn, swa_w_qkv[s].astype(BF16), gain, cos, sin, nq + nkv, rope=True)
            att = dict(kind="swa", k_col0=N_HEADS, v_col0=N_HEADS + N_KV_HEADS, group=GQA_GROUP,
                       sink=swa_sink[s], **common)
            o = _attention(qkv, latent=True, **att)
            if ctx_out:
                o = jnp.concatenate([o, _attention(qkv, latent=False, **att)], axis=0)
            w_o = swa_w_o[s]
        elif kind == 1:
            o = _fourier_mix(xn, all_rows, n_batch, seq, ctx_len, ctx_out)
            w_o = fnet_w_o[s]
        else:
            width = N_HEADS * HEAD_DIM
            gain = jnp.concatenate([jnp.tile(nat_q_gain[s], N_HEADS), jnp.tile(nat_k_gain[s], N_HEADS),
                                    ones_row])[None, :]
            qkv = _qkv_proj(xn, nat_w_qkv[s].astype(BF16), gain, cos, sin, 2 * width, rope=False)
            att = dict(kind="nat", k_col0=N_HEADS, v_col0=2 * N_HEADS, group=1, **common)
            o = _attention(qkv, latent=True, bias=_nat_bias_table(nat_rpb[s]), **att)
            if ctx_out:
                o = jnp.concatenate([o, _attention(qkv, latent=False, **att)], axis=0)
            w_o = nat_w_o[s]
        h = _mm_residual(o, w_o.astype(BF16), h, mods, 2, rows, seq, n_batch)
        h = _moe(h, norm2_g[layer], mods, rows, seq, n_batch, moe_router[layer], moe_bias[layer],
                 moe_w_gate[layer], moe_w_up[layer], moe_w_down[layer],
                 shared_w_gate[layer], shared_w_up[layer], shared_w_down[layer])
    return h[:lat_rows].reshape(n_batch, seq, D_MODEL)
```

```python
import functools

import numpy as np
import jax
import jax.numpy as jnp
from jax import lax
from jax.experimental import pallas as pl
from jax.experimental.pallas import tpu as pltpu

F32 = jnp.float32
BF16 = jnp.bfloat16
U32 = jnp.uint32
I32 = jnp.int32

D_MODEL = 4096
DEPTH = 4
GRID_W = 64
HEAD_DIM = 128
N_HEADS = 32
N_KV_HEADS = 8
GQA_GROUP = N_HEADS // N_KV_HEADS
WINDOW = 128
ROPE_THETA = 10000.0
ROPE_AXIS_DIM = HEAD_DIM // 2
NAT_KH = 8
NAT_KW = 16
NAT_ROWS_PER_BLOCK = 4
NAT_BLOCK = NAT_ROWS_PER_BLOCK * GRID_W
FOURIER_GROUPS = 8
FOURIER_GROUP_DIM = D_MODEL // FOURIER_GROUPS
N_EXPERTS = 32
TOP_K = 4
N_EXPERT_GROUPS = 4
GROUP_SIZE = N_EXPERTS // N_EXPERT_GROUPS
TOPK_GROUPS = 2
EXPERT_HIDDEN = 256
SHARED_HIDDEN = 1024
ROUTED_SCALE = 2.5
ADA_RANK = 256
N_MOD = 6
EPS = 1e-6
MASKED = -1e30

VMEM_LIMIT_BYTES = 56 * 1024 * 1024
ROW_TILE = 1024
COL_TILE = 512
EXPERT_TILE = 256
TOKEN_TILE = 256
HALF_D = D_MODEL // 2
LANES = 128
SLAB = HALF_D // LANES
COND_ROWS = 16
DMA_UNROLL = 8


def _params(*sem, **kw):
    return pltpu.CompilerParams(dimension_semantics=sem, vmem_limit_bytes=VMEM_LIMIT_BYTES, **kw)


def _dot(a, b):
    return jnp.dot(a, b, preferred_element_type=F32)


def _dot_nt(a, b):
    return lax.dot_general(a, b, (((1,), (1,)), ((), ())), preferred_element_type=F32)


def _silu(x):
    return x * (1.0 / (1.0 + jnp.exp(-x)))


def _pack_halves(lo, hi):
    lo_bits = lax.bitcast_convert_type(lo.astype(BF16).astype(F32), U32)
    hi_bits = lax.bitcast_convert_type(hi.astype(BF16).astype(F32), U32)
    return (lo_bits >> 16) | hi_bits


def _unpack_halves(word):
    lo = lax.bitcast_convert_type(word << 16, F32)
    hi = lax.bitcast_convert_type(word & jnp.uint32(0xFFFF0000), F32)
    return lo, hi


def _slab_rows(first_token, n_tokens, part):
    return pl.ds(first_token * SLAB + part, n_tokens, stride=SLAB)


def _store_slabs(ref, packed):
    n = packed.shape[0]
    for part in range(SLAB):
        ref[_slab_rows(0, n, part), :] = packed[:, part * LANES:(part + 1) * LANES]


def _load_slabs(ref, n):
    return jnp.concatenate([ref[_slab_rows(0, n, part), :] for part in range(SLAB)], axis=1)


def _ada_kernel(cond_ref, down_ref, up_ref, bias_ref, o_ref):
    c = cond_ref[...]
    z = _dot(_silu(c).astype(BF16), down_ref[0].astype(BF16))
    o_ref[0] = _dot(z.astype(BF16), up_ref[0].astype(BF16)) + bias_ref[0]


def _ada_mods(cond, ada_down, ada_up, ada_bias):
    return pl.pallas_call(
        _ada_kernel,
        grid=(DEPTH, N_MOD),
        in_specs=[
            pl.BlockSpec((COND_ROWS, D_MODEL), lambda l, j: (0, 0)),
            pl.BlockSpec((1, D_MODEL, ADA_RANK), lambda l, j: (l, 0, 0)),
            pl.BlockSpec((1, ADA_RANK, D_MODEL), lambda l, j: (l, 0, j)),
            pl.BlockSpec((1, 1, D_MODEL), lambda l, j: (l, 0, j)),
        ],
        out_specs=pl.BlockSpec((1, COND_ROWS, D_MODEL), lambda l, j: (l, 0, j)),
        out_shape=jax.ShapeDtypeStruct((DEPTH, COND_ROWS, N_MOD * D_MODEL), F32),
        compiler_params=_params("arbitrary", "arbitrary"),
        name="ada_mods",
    )(cond, ada_down, ada_up, ada_bias[:, None, :])


def _norm_mod_tile(h_ref, g_ref, shift_ref, scale_ref):
    x = h_ref[...]
    ms = jnp.mean(x * x, axis=-1, keepdims=True)
    y = x * lax.rsqrt(ms + EPS) * g_ref[...]
    return y * (1.0 + scale_ref[0, 0]) + shift_ref[0, 0]


def _norm_mod_kernel(h_ref, g_ref, shift_ref, scale_ref, o_ref):
    o_ref[...] = _norm_mod_tile(h_ref, g_ref, shift_ref, scale_ref).astype(BF16)


def _split_bf16(x):
    hi = x.astype(BF16)
    lo = (x - hi.astype(F32)).astype(BF16)
    return hi, lo


def _first_index_of_max(v, idx, n):
    m = jnp.max(v, axis=0, keepdims=True)
    i = jnp.min(jnp.where(v == m, idx, n), axis=0, keepdims=True)
    return m, i


def _norm_route_kernel(h_ref, g_ref, shift_ref, scale_ref, wr_ref, eb_ref,
                       o_ref, xw_ref, idx_ref, wts_ref, rank_ref, cnt_ref, run_ref):
    @pl.when(pl.program_id(0) == 0)
    def _():
        run_ref[...] = jnp.zeros_like(run_ref)

    xn = _norm_mod_tile(h_ref, g_ref, shift_ref, scale_ref)
    o_ref[...] = xn.astype(BF16)
    _store_slabs(xw_ref, _pack_halves(xn[:, :HALF_D], xn[:, HALF_D:]))
    x_hi, x_lo = _split_bf16(xn)
    w_hi, w_lo = _split_bf16(wr_ref[...])
    logits = _dot(x_hi, w_hi) + _dot(x_lo, w_hi) + _dot(x_hi, w_lo)
    lt = logits.T[:N_EXPERTS]
    scores = 1.0 / (1.0 + jnp.exp(-lt))
    sel = scores + eb_ref[...]
    n_tok = sel.shape[1]
    sub = lax.broadcasted_iota(I32, (GROUP_SIZE, n_tok), 0).astype(F32)
    grp_scores = []
    for g in range(N_EXPERT_GROUPS):
        sg = sel[g * GROUP_SIZE:(g + 1) * GROUP_SIZE]
        m1, i1 = _first_index_of_max(sg, sub, float(GROUP_SIZE))
        m2 = jnp.max(jnp.where(sub == i1, -jnp.inf, sg), axis=0, keepdims=True)
        grp_scores.append(m1 + m2)
    eidx_i = lax.broadcasted_iota(I32, sel.shape, 0)
    eidx = eidx_i.astype(F32)
    egrp = (eidx_i // GROUP_SIZE).astype(F32)
    best = jnp.full((1, n_tok), -jnp.inf, F32)
    for g in range(N_EXPERT_GROUPS):
        best = jnp.maximum(best, grp_scores[g])
    g1 = jnp.full((1, n_tok), float(N_EXPERT_GROUPS), F32)
    for g in reversed(range(N_EXPERT_GROUPS)):
        g1 = jnp.where(grp_scores[g] == best, float(g), g1)
    second = jnp.full((1, n_tok), -jnp.inf, F32)
    for g in range(N_EXPERT_GROUPS):
        second = jnp.maximum(second, jnp.where(g1 == float(g), -jnp.inf, grp_scores[g]))
    g2 = jnp.full((1, n_tok), float(N_EXPERT_GROUPS), F32)
    for g in reversed(range(N_EXPERT_GROUPS)):
        g2 = jnp.where((grp_scores[g] == second) & (g1 != float(g)), float(g), g2)
    masked = jnp.where((egrp == g1) | (egrp == g2), sel, -jnp.inf)
    picks, weights, hits = [], [], []
    for _ in range(TOP_K):
        _, ik = _first_index_of_max(masked, eidx, float(N_EXPERTS))
        hit = eidx == ik
        weights.append(jnp.sum(jnp.where(hit, scores, 0.0), axis=0, keepdims=True))
        picks.append(ik)
        hits.append(hit)
        masked = jnp.where(hit, -jnp.inf, masked)
    w = jnp.concatenate(weights, axis=0)
    idx_ref[...] = jnp.concatenate(picks, axis=0).astype(I32)
    wts_ref[...] = w / jnp.sum(w, axis=0, keepdims=True) * ROUTED_SCALE
    member = jnp.zeros(sel.shape, F32)
    for hit in hits:
        member = jnp.where(hit, 1.0, member)
    earlier = lax.broadcasted_iota(I32, (n_tok, n_tok), 0) < lax.broadcasted_iota(I32, (n_tok, n_tok), 1)
    before = run_ref[...] + _dot(member.astype(BF16), jnp.where(earlier, 1.0, 0.0).astype(BF16))
    rank_ref[...] = jnp.concatenate(
        [jnp.sum(jnp.where(hit, before, 0.0), axis=0, keepdims=True) for hit in hits], axis=0).astype(I32)
    run_ref[...] = run_ref[...] + jnp.sum(member, axis=1, keepdims=True)
    cnt_ref[...] = run_ref[...]


def _seg_of_tile(i, tm, seq, n_batch):
    return jnp.minimum((i * tm) // seq, n_batch)


def _norm_mod(h, g, mods, shift_k, scale_k, rows, seq, n_batch, router=None):
    tm = TOKEN_TILE
    seg = functools.partial(_seg_of_tile, tm=tm, seq=seq, n_batch=n_batch)
    in_specs = [
        pl.BlockSpec((tm, D_MODEL), lambda i: (i, 0)),
        pl.BlockSpec((1, D_MODEL), lambda i: (0, 0)),
        pl.BlockSpec((1, 1, 1, D_MODEL), lambda i: (seg(i), shift_k, 0, 0)),
        pl.BlockSpec((1, 1, 1, D_MODEL), lambda i: (seg(i), scale_k, 0, 0)),
    ]
    xn_spec = pl.BlockSpec((tm, D_MODEL), lambda i: (i, 0))
    xn_shape = jax.ShapeDtypeStruct((rows, D_MODEL), BF16)
    if router is None:
        return pl.pallas_call(
            _norm_mod_kernel, grid=(rows // tm,), in_specs=in_specs, out_specs=xn_spec, out_shape=xn_shape,
            compiler_params=_params("arbitrary"), name="norm_mod",
        )(h, g.reshape(1, D_MODEL), mods, mods)
    w_router, e_bias = router
    w_pad = jnp.pad(w_router, ((0, 0), (0, 128 - N_EXPERTS)))
    in_specs += [
        pl.BlockSpec((D_MODEL, 128), lambda i: (0, 0)),
        pl.BlockSpec((N_EXPERTS, 1), lambda i: (0, 0)),
    ]
    per_tok = pl.BlockSpec((TOP_K, tm), lambda i: (0, i))
    return pl.pallas_call(
        _norm_route_kernel, grid=(rows // tm,), in_specs=in_specs,
        out_specs=[xn_spec, pl.BlockSpec((tm * SLAB, LANES), lambda i: (i, 0)), per_tok, per_tok, per_tok,
                   pl.BlockSpec((N_EXPERTS, 1), lambda i: (0, 0))],
        out_shape=[xn_shape, jax.ShapeDtypeStruct((rows * SLAB, LANES), U32),
                   jax.ShapeDtypeStruct((TOP_K, rows), I32), jax.ShapeDtypeStruct((TOP_K, rows), F32),
                   jax.ShapeDtypeStruct((TOP_K, rows), I32), jax.ShapeDtypeStruct((N_EXPERTS, 1), F32)],
        scratch_shapes=[pltpu.VMEM((N_EXPERTS, 1), F32)],
        compiler_params=_params("arbitrary"), name="norm_route",
    )(h, g.reshape(1, D_MODEL), mods, mods, w_pad, e_bias.reshape(N_EXPERTS, 1))


def _cast_weight_once(w_ref, w_scr):
    @pl.when(pl.program_id(1) == 0)
    def _():
        w_scr[...] = w_ref[0].astype(BF16)


def _qkv_kernel(a_ref, w_ref, gain_ref, cos_ref, sin_ref, o_ref, w_scr, *, n_qk_tiles, rope):
    _cast_weight_once(w_ref, w_scr)
    j = pl.program_id(0)
    acc = _dot(a_ref[...], w_scr[...])
    n_heads = acc.shape[1] // HEAD_DIM

    @pl.when(j < n_qk_tiles)
    def _():
        lane = lax.broadcasted_iota(I32, (acc.shape[0], HEAD_DIM), 1)
        quarter = ROPE_AXIS_DIM // 2
        first_half = (lane % ROPE_AXIS_DIM) < quarter
        outs = []
        for hh in range(n_heads):
            x = acc[:, hh * HEAD_DIM:(hh + 1) * HEAD_DIM]
            ms = jnp.mean(x * x, axis=-1, keepdims=True)
            y = x * lax.rsqrt(ms + EPS) * gain_ref[:, hh * HEAD_DIM:(hh + 1) * HEAD_DIM]
            if rope:
                partner = jnp.where(first_half, pltpu.roll(y, HEAD_DIM - quarter, 1), pltpu.roll(y, quarter, 1))
                y = y * cos_ref[...] + partner * sin_ref[...]
            outs.append(y.astype(BF16))
        o_ref[...] = jnp.concatenate(outs, axis=1)

    @pl.when(j >= n_qk_tiles)
    def _():
        o_ref[...] = acc.astype(BF16)


def _qkv_proj(xn, w_stack, w_slot, gain_row, cos, sin, n_qk_cols, rope):
    rows, width = xn.shape[0], w_stack.shape[2]
    tm, tn = ROW_TILE, COL_TILE
    return pl.pallas_call(
        functools.partial(_qkv_kernel, n_qk_tiles=n_qk_cols // tn, rope=rope),
        grid=(width // tn, rows // tm),
        in_specs=[
            pl.BlockSpec((tm, D_MODEL), lambda j, i: (i, 0)),
            pl.BlockSpec((1, D_MODEL, tn), lambda j, i: (w_slot, 0, j)),
            pl.BlockSpec((1, tn), lambda j, i: (0, j)),
            pl.BlockSpec((tm, HEAD_DIM), lambda j, i: (i, 0)),
            pl.BlockSpec((tm, HEAD_DIM), lambda j, i: (i, 0)),
        ],
        out_specs=pl.BlockSpec((tm, tn), lambda j, i: (i, j)),
        out_shape=jax.ShapeDtypeStruct((rows, width), BF16),
        scratch_shapes=[pltpu.VMEM((D_MODEL, tn), BF16)],
        compiler_params=_params("arbitrary", "arbitrary"),
        name="qkv_proj",
    )(xn, w_stack, gain_row, cos, sin)


def _mm_res_kernel(a_ref, w_ref, h_ref, gate_ref, o_ref, w_scr):
    _cast_weight_once(w_ref, w_scr)
    o_ref[...] = h_ref[...] + gate_ref[0, 0] * _dot(a_ref[...], w_scr[...])


def _mm_residual(a, w_stack, w_slot, h, mods, gate_k, rows, seq, n_batch):
    _, k_dim, width = w_stack.shape
    tm, tn = ROW_TILE, COL_TILE
    seg = functools.partial(_seg_of_tile, tm=tm, seq=seq, n_batch=n_batch)
    return pl.pallas_call(
        _mm_res_kernel,
        grid=(width // tn, rows // tm),
        in_specs=[
            pl.BlockSpec((tm, k_dim), lambda j, i: (i, 0)),
            pl.BlockSpec((1, k_dim, tn), lambda j, i: (w_slot, 0, j)),
            pl.BlockSpec((tm, tn), lambda j, i: (i, j)),
            pl.BlockSpec((1, 1, 1, tn), lambda j, i: (seg(i), gate_k, 0, j)),
        ],
        out_specs=pl.BlockSpec((tm, tn), lambda j, i: (i, j)),
        out_shape=jax.ShapeDtypeStruct((rows, width), F32),
        scratch_shapes=[pltpu.VMEM((k_dim, tn), BF16)],
        compiler_params=_params("arbitrary", "arbitrary"),
        name="mm_residual",
    )(a, w_stack, h, mods)


def _swiglu_kernel(a_ref, wg_ref, wu_ref, o_ref, wg_scr, wu_scr):
    _cast_weight_once(wg_ref, wg_scr)
    _cast_weight_once(wu_ref, wu_scr)
    a = a_ref[...]
    o_ref[...] = (_silu(_dot(a, wg_scr[...])) * _dot(a, wu_scr[...])).astype(BF16)


def _swiglu(a, wg_stack, wu_stack, layer, rows):
    width = wg_stack.shape[2]
    tm, tn = ROW_TILE, COL_TILE // 2
    w_spec = pl.BlockSpec((1, D_MODEL, tn), lambda j, i: (layer, 0, j))
    return pl.pallas_call(
        _swiglu_kernel,
        grid=(width // tn, rows // tm),
        in_specs=[pl.BlockSpec((tm, D_MODEL), lambda j, i: (i, 0)), w_spec, w_spec],
        out_specs=pl.BlockSpec((tm, tn), lambda j, i: (i, j)),
        out_shape=jax.ShapeDtypeStruct((rows, width), BF16),
        scratch_shapes=[pltpu.VMEM((D_MODEL, tn), BF16), pltpu.VMEM((D_MODEL, tn), BF16)],
        compiler_params=_params("arbitrary", "arbitrary"),
        name="shared_swiglu",
    )(a, wg_stack, wu_stack)


def _attn_kernel(*refs, n_local, group, block, kind, has_sink, has_prev, n_blocks):
    refs = list(refs)
    sink_ref = refs.pop(0) if has_sink else None
    q_ref = refs.pop(0)
    k_loc = [refs.pop(0) for _ in range(n_local)]
    v_loc = [refs.pop(0) for _ in range(n_local)]
    kc_ref, vc_ref = refs.pop(0), refs.pop(0)
    bias_ref = refs.pop(0) if kind == "nat" else None
    if has_prev:
        refs.pop(0)
    o_ref = refs.pop(0)
    head = pl.program_id(1)
    i = pl.program_id(2)
    scale = HEAD_DIM ** -0.5

    q = q_ref[...]
    if group > 1:
        q = jnp.concatenate([q[:, g * HEAD_DIM:(g + 1) * HEAD_DIM] for g in range(group)], axis=0)
    n_rows = q.shape[0]
    s_ctx = _dot_nt(q, kc_ref[...]) * scale
    m = jnp.max(s_ctx, axis=-1, keepdims=True)
    if n_local:
        k = jnp.concatenate([r[...] for r in k_loc], axis=0)
        v = jnp.concatenate([r[...] for r in v_loc], axis=0)
        s_loc = _dot_nt(q, k) * scale
        row = lax.broadcasted_iota(I32, (n_rows, 1), 0)
        col = lax.broadcasted_iota(I32, (1, n_local * block), 1)
        if kind == "swa":
            rel = col - block - (row % block)
            lo = jnp.where(i == 0, block, 0)
            hi = jnp.where(i == n_blocks - 1, 2 * block, 3 * block)
            valid = (rel >= -WINDOW) & (rel <= WINDOW) & (col >= lo) & (col < hi)
        else:
            q_row = NAT_ROWS_PER_BLOCK * i + row // GRID_W
            k_row = NAT_ROWS_PER_BLOCK * (i - 1) + col // GRID_W
            r0 = jnp.clip(q_row - NAT_KH // 2, 0, n_blocks * NAT_ROWS_PER_BLOCK - NAT_KH)
            valid = (k_row >= r0) & (k_row < r0 + NAT_KH)
            s_loc = s_loc + bias_ref[0]
        s_loc = jnp.where(valid, s_loc, MASKED)
        m = jnp.maximum(m, jnp.max(s_loc, axis=-1, keepdims=True))
    if has_sink:
        g_of_row = lax.broadcasted_iota(I32, (n_rows, 1), 0) // (n_rows // group)
        sink = jnp.zeros((n_rows, 1), F32)
        for g in range(group):
            sink = jnp.where(g_of_row == g, sink_ref[head * group + g], sink)
        m = jnp.maximum(m, sink)
    p_ctx = jnp.exp(s_ctx - m)
    denom = jnp.sum(p_ctx, axis=-1, keepdims=True)
    o = _dot(p_ctx.astype(BF16), vc_ref[...])
    if n_local:
        p_loc = jnp.exp(s_loc - m)
        denom = denom + jnp.sum(p_loc, axis=-1, keepdims=True)
        o = o + _dot(p_loc.astype(BF16), v)
    if has_sink:
        denom = denom + jnp.exp(sink - m)
    o = o * (1.0 / denom)
    if group > 1:
        rows_per = n_rows // group
        o = jnp.concatenate([o[g * rows_per:(g + 1) * rows_per] for g in range(group)], axis=1)
    o_ref[...] = o.astype(BF16)


def _attention(qkv, *, kind, latent, n_batch, seq, ctx_len, k_col0, v_col0, group, out_rows,
               sink=None, bias=None, prev=None):
    n_kv = v_col0 - k_col0
    block = (WINDOW if kind == "swa" else NAT_BLOCK) if latent else ctx_len
    n_blocks = seq // block if latent else 1
    n_local = 3 if latent else 0
    ctx_blk0 = n_batch * seq // ctx_len
    q_blk0 = 0 if latent else n_batch * seq // block
    has_sink = sink is not None
    nat_order = kind == "nat"

    def ids(a, b, c):
        return (b, a, c) if nat_order else (a, b, c)

    def q_map(a, b, c):
        bb, hh, ii = ids(a, b, c)
        return (q_blk0 + bb * n_blocks + ii, hh)

    def loc_map(delta, col0):
        def f(a, b, c):
            bb, hh, ii = ids(a, b, c)
            return (bb * n_blocks + jnp.clip(ii + delta, 0, n_blocks - 1), col0 + hh)
        return f

    def ctx_map(col0):
        def f(a, b, c):
            bb, hh, _ = ids(a, b, c)
            return (ctx_blk0 + bb, col0 + hh)
        return f

    in_specs, args = [], []
    if has_sink:
        in_specs.append(pl.BlockSpec(memory_space=pltpu.SMEM))
        args.append(sink.astype(F32))
    in_specs.append(pl.BlockSpec((block, group * HEAD_DIM), q_map))
    args.append(qkv)
    for col0 in ((k_col0, v_col0) if latent else ()):
        for delta in (-1, 0, 1):
            in_specs.append(pl.BlockSpec((block, HEAD_DIM), loc_map(delta, col0)))
            args.append(qkv)
    for col0 in (k_col0, v_col0):
        in_specs.append(pl.BlockSpec((ctx_len, HEAD_DIM), ctx_map(col0)))
        args.append(qkv)
    use_bias = nat_order and latent
    if use_bias:
        in_specs.append(pl.BlockSpec((1, block, 3 * block), lambda a, b, c: (a, 0, 0)))
        args.append(bias)
    aliases = {}
    if prev is not None:
        aliases = {len(args): 0}
        in_specs.append(pl.BlockSpec(memory_space=pl.ANY))
        args.append(prev)
    grid = (n_kv, n_batch, n_blocks) if nat_order else (n_batch, n_kv, n_blocks)
    kern = functools.partial(_attn_kernel, n_local=n_local, group=group, block=block,
                             kind=("nat" if use_bias else kind if latent else "ctx"),
                             has_sink=has_sink, has_prev=prev is not None, n_blocks=n_blocks)
    return pl.pallas_call(
        kern, grid=grid, in_specs=in_specs,
        out_specs=pl.BlockSpec((block, group * HEAD_DIM), q_map),
        out_shape=jax.ShapeDtypeStruct((out_rows, n_kv * group * HEAD_DIM), BF16),
        input_output_aliases=aliases,
        compiler_params=_params("arbitrary", "arbitrary", "arbitrary"),
        name=f"attn_{kind}_{'latent' if latent else 'ctx'}",
    )(*args)


def _nat_bias_table(rpb):
    q = np.arange(NAT_BLOCK)
    k = np.arange(3 * NAT_BLOCK)
    q_r, q_c = q // GRID_W, q % GRID_W
    k_r, k_c = k // GRID_W, k % GRID_W
    d_row = k_r[None, :] - NAT_ROWS_PER_BLOCK - q_r[:, None] + NAT_KH - 1
    d_col = np.clip(k_c[None, :] - q_c[:, None] + NAT_KW - 1, 0, 2 * NAT_KW - 2)
    c0 = np.clip(q_c - NAT_KW // 2, 0, GRID_W - NAT_KW)
    col_valid = (k_c[None, :] >= c0[:, None]) & (k_c[None, :] < c0[:, None] + NAT_KW)
    bias = rpb.astype(F32)[:, d_row, d_col]
    return jnp.where(col_valid[None], bias, MASKED)


def _dft_chan_kernel(a_ref, w_ref, o_ref):
    o_ref[0] = _dot(a_ref[...], w_ref[0]).astype(BF16)


def _dft_pos_kernel(cn_ref, sn_ref, xc_ref, xs_ref, *rest, scale):
    o_ref = rest[-1]
    o_ref[...] = ((_dot(cn_ref[...], xc_ref[0]) - _dot(sn_ref[...], xs_ref[0])) * scale).astype(BF16)


def _dft_tables(n):
    k = jnp.arange(n, dtype=I32)
    ang = ((k[:, None] * k[None, :]) % n).astype(F32) * (2.0 * np.pi / n)
    return jnp.cos(ang).astype(BF16), jnp.sin(ang).astype(BF16)


def _fourier_mix(xn, rows, n_batch, seq, ctx_len, ctx_out):
    gd = FOURIER_GROUP_DIM
    cd, sd = _dft_tables(gd)
    chan = jnp.stack([cd, sd])
    tm = ROW_TILE
    xcs = pl.pallas_call(
        _dft_chan_kernel,
        grid=(rows // tm, FOURIER_GROUPS, 2),
        in_specs=[pl.BlockSpec((tm, gd), lambda i, g, t: (i, g)),
                  pl.BlockSpec((1, gd, gd), lambda i, g, t: (t, 0, 0))],
        out_specs=pl.BlockSpec((1, tm, gd), lambda i, g, t: (t, i, g)),
        out_shape=jax.ShapeDtypeStruct((2, rows, D_MODEL), BF16),
        compiler_params=_params("arbitrary", "arbitrary", "arbitrary"),
        name="dft_channels",
    )(xn, chan)

    def pos_call(n, tmo, row_blk0, in_blk0, out_rows, prev):
        cn, sn = _dft_tables(n)
        tn = COL_TILE
        per = n // tmo
        in_specs = [
            pl.BlockSpec((tmo, n), lambda b, i, j: (i, 0)),
            pl.BlockSpec((tmo, n), lambda b, i, j: (i, 0)),
            pl.BlockSpec((1, n, tn), lambda b, i, j: (0, in_blk0 + b, j)),
            pl.BlockSpec((1, n, tn), lambda b, i, j: (1, in_blk0 + b, j)),
        ]
        args = [cn, sn, xcs, xcs]
        aliases = {}
        if prev is not None:
            in_specs.append(pl.BlockSpec(memory_space=pl.ANY))
            args.append(prev)
            aliases = {4: 0}
        return pl.pallas_call(
            functools.partial(_dft_pos_kernel, scale=float((n * gd) ** -0.5)),
            grid=(n_batch, per, D_MODEL // tn),
            in_specs=in_specs,
            out_specs=pl.BlockSpec((tmo, tn), lambda b, i, j: (row_blk0 + b * per + i, j)),
            out_shape=jax.ShapeDtypeStruct((out_rows, D_MODEL), BF16),
            input_output_aliases=aliases,
            compiler_params=_params("arbitrary", "arbitrary", "arbitrary"),
            name=f"dft_positions_{n}",
        )(*args)

    z = pos_call(seq, 512, 0, 0, rows, None)
    if ctx_out:
        z = pos_call(ctx_len, ctx_len, n_batch * seq // ctx_len, n_batch * seq // ctx_len, rows, z)
    return z


def _slab_copy(src_ref, src_token, dst_ref, dst_token, sem):
    src = src_ref.at[pl.ds(pl.multiple_of(src_token * SLAB, SLAB), SLAB)]
    dst = dst_ref.at[pl.ds(pl.multiple_of(dst_token * SLAB, SLAB), SLAB)]
    return pltpu.make_async_copy(src, dst, sem)


def _load_slots(pos_ref, pos_smem, sem):
    cp = pltpu.make_async_copy(pos_ref.at[0, 0], pos_smem, sem)
    cp.start()
    cp.wait()


def _for_token_groups(n_tokens, fn):
    def group(g, c):
        for k in range(TOP_K):
            for u in range(DMA_UNROLL):
                fn(k, g * DMA_UNROLL + u)
        return c
    lax.fori_loop(0, n_tokens // DMA_UNROLL, group, 0)


def _dispatch_kernel(cnt_ref, off_ref, pos_ref, xw_ref, xs_ref, pos_smem, zero_ref, sem_idx, sem_rows):
    tm = xw_ref.shape[0] // SLAB

    @pl.when(pl.program_id(0) == 0)
    def _():
        zero_ref[...] = jnp.zeros_like(zero_ref)

        def per_expert(e, carry):
            count = cnt_ref[e]
            first = off_ref[e] + count
            n_pad = (count + EXPERT_TILE - 1) // EXPERT_TILE * EXPERT_TILE - count

            def start(r, c):
                _slab_copy(zero_ref, 0, xs_ref, first + r, sem_rows).start()
                return c

            def wait(r, c):
                _slab_copy(zero_ref, 0, xs_ref, first + r, sem_rows).wait()
                return c

            lax.fori_loop(0, n_pad, start, 0)
            lax.fori_loop(0, n_pad, wait, 0)
            return carry

        lax.fori_loop(0, N_EXPERTS, per_expert, 0)

    _load_slots(pos_ref, pos_smem, sem_idx)
    _for_token_groups(tm, lambda k, t: _slab_copy(xw_ref, t, xs_ref, pos_smem[k * tm + t], sem_rows).start())
    _for_token_groups(tm, lambda k, t: _slab_copy(xw_ref, t, xs_ref, pos_smem[k * tm + t], sem_rows).wait())


def _dispatch(xw, pos_tiles, counts, offs, n_slots):
    rows = xw.shape[0] // SLAB
    tm = TOKEN_TILE
    grid_spec = pltpu.PrefetchScalarGridSpec(
        num_scalar_prefetch=2,
        grid=(rows // tm,),
        in_specs=[
            pl.BlockSpec((1, 1, TOP_K * tm), lambda i, cnt, off: (i, 0, 0)),
            pl.BlockSpec((tm * SLAB, LANES), lambda i, cnt, off: (i, 0)),
        ],
        out_specs=pl.BlockSpec(memory_space=pl.ANY),
        scratch_shapes=[
            pltpu.SMEM((TOP_K * tm,), I32),
            pltpu.VMEM((SLAB, LANES), U32),
            pltpu.SemaphoreType.DMA,
            pltpu.SemaphoreType.DMA,
        ],
    )
    return pl.pallas_call(
        _dispatch_kernel, grid_spec=grid_spec,
        out_shape=jax.ShapeDtypeStruct((n_slots * SLAB, LANES), U32),
        compiler_params=_params("arbitrary", disable_bounds_checks=True),
        name="moe_dispatch",
    )(counts, offs, pos_tiles, xw)


def _expert_kernel(te_ref, nu_ref, x_ref, wg_ref, wu_ref, wd_ref, o_ref, wg_scr, wu_scr, wd_scr):
    j = pl.program_id(0)

    @pl.when(j < nu_ref[0])
    def _():
        @pl.when((j == 0) | (te_ref[j] != te_ref[jnp.maximum(j - 1, 0)]))
        def _():
            wg_scr[...] = wg_ref[0, 0].astype(BF16)
            wu_scr[...] = wu_ref[0, 0].astype(BF16)
            wd_scr[...] = wd_ref[0, 0].astype(BF16)

        x_lo, x_hi = _unpack_halves(_load_slabs(x_ref, EXPERT_TILE))
        x_lo, x_hi = x_lo.astype(BF16), x_hi.astype(BF16)
        gate = _dot(x_lo, wg_scr[:HALF_D]) + _dot(x_hi, wg_scr[HALF_D:])
        up = _dot(x_lo, wu_scr[:HALF_D]) + _dot(x_hi, wu_scr[HALF_D:])
        hid = (_silu(gate) * up).astype(BF16)
        _store_slabs(o_ref, _pack_halves(_dot(hid, wd_scr[:, :HALF_D]), _dot(hid, wd_scr[:, HALF_D:])))


def _experts(x_sorted, tile_expert, n_used, wg, wu, wd, layer):
    n_rows = x_sorted.shape[0] // SLAB
    tm = EXPERT_TILE

    def row_map(j, te, nu):
        return (jnp.minimum(j, nu[0] - 1), 0)

    def w_map(j, te, nu):
        return (layer, te[j], 0, 0)

    grid_spec = pltpu.PrefetchScalarGridSpec(
        num_scalar_prefetch=2,
        grid=(n_rows // tm,),
        in_specs=[
            pl.BlockSpec((tm * SLAB, LANES), row_map),
            pl.BlockSpec((1, 1, D_MODEL, EXPERT_HIDDEN), w_map),
            pl.BlockSpec((1, 1, D_MODEL, EXPERT_HIDDEN), w_map),
            pl.BlockSpec((1, 1, EXPERT_HIDDEN, D_MODEL), w_map),
        ],
        out_specs=pl.BlockSpec((tm * SLAB, LANES), row_map),
        scratch_shapes=[
            pltpu.VMEM((D_MODEL, EXPERT_HIDDEN), BF16),
            pltpu.VMEM((D_MODEL, EXPERT_HIDDEN), BF16),
            pltpu.VMEM((EXPERT_HIDDEN, D_MODEL), BF16),
        ],
    )
    return pl.pallas_call(
        _expert_kernel, grid_spec=grid_spec,
        out_shape=jax.ShapeDtypeStruct((n_rows * SLAB, LANES), U32),
        compiler_params=_params("arbitrary"),
        name="routed_experts",
    )(tile_expert, n_used, x_sorted, wg, wu, wd)


def _combine_kernel(pos_ref, h_ref, w_ref, gate_ref, y_ref, o_ref, pos_smem, buf_ref, sem_idx, sem_rows):
    tm = h_ref.shape[0]
    _load_slots(pos_ref, pos_smem, sem_idx)
    _for_token_groups(tm, lambda k, t: _slab_copy(y_ref, pos_smem[k * tm + t], buf_ref.at[k], t, sem_rows).start())
    _for_token_groups(tm, lambda k, t: _slab_copy(y_ref, pos_smem[k * tm + t], buf_ref.at[k], t, sem_rows).wait())
    sub = 8

    def rows_chunk(r, c):
        first = pl.multiple_of(r * sub, sub)
        rs = pl.ds(first, sub)
        w = w_ref[rs, :]
        wk = [jnp.broadcast_to(w[:, k:k + 1], (sub, LANES)) for k in range(TOP_K)]
        for part in range(SLAB):
            acc_lo = jnp.zeros((sub, LANES), F32)
            acc_hi = jnp.zeros((sub, LANES), F32)
            for k in range(TOP_K):
                y_lo, y_hi = _unpack_halves(buf_ref[k, _slab_rows(first, sub, part), :])
                acc_lo = acc_lo + wk[k] * y_lo
                acc_hi = acc_hi + wk[k] * y_hi
            lo = pl.ds(part * LANES, LANES)
            hi = pl.ds(HALF_D + part * LANES, LANES)
            o_ref[rs, lo] = h_ref[rs, lo] + gate_ref[0, 0, :, lo] * acc_lo
            o_ref[rs, hi] = h_ref[rs, hi] + gate_ref[0, 0, :, hi] * acc_hi
        return c

    lax.fori_loop(0, tm // sub, rows_chunk, 0)


def _combine(h, y_sorted, pos_tiles, wts_t, mods, gate_k, rows, seq, n_batch):
    tm = TOKEN_TILE
    seg = functools.partial(_seg_of_tile, tm=tm, seq=seq, n_batch=n_batch)
    return pl.pallas_call(
        _combine_kernel,
        grid=(rows // tm,),
        in_specs=[
            pl.BlockSpec((1, 1, TOP_K * tm), lambda i: (i, 0, 0)),
            pl.BlockSpec((tm, D_MODEL), lambda i: (i, 0)),
            pl.BlockSpec((tm, TOP_K), lambda i: (i, 0)),
            pl.BlockSpec((1, 1, 1, D_MODEL), lambda i: (seg(i), gate_k, 0, 0)),
            pl.BlockSpec(memory_space=pl.ANY),
        ],
        out_specs=pl.BlockSpec((tm, D_MODEL), lambda i: (i, 0)),
        out_shape=jax.ShapeDtypeStruct((rows, D_MODEL), F32),
        scratch_shapes=[
            pltpu.SMEM((TOP_K * tm,), I32),
            pltpu.VMEM((TOP_K, tm * SLAB, LANES), U32),
            pltpu.SemaphoreType.DMA,
            pltpu.SemaphoreType.DMA,
        ],
        compiler_params=_params("arbitrary", disable_bounds_checks=True),
        name="moe_combine",
    )(pos_tiles, h, wts_t, mods, y_sorted)


def _dispatch_plan(idx, rank, counts_f, rows):
    tm = EXPERT_TILE
    counts = counts_f[:, 0].astype(I32)
    padded = (counts + tm - 1) // tm * tm
    ends = jnp.cumsum(padded)
    offs = ends - padded
    expert = jnp.arange(N_EXPERTS, dtype=I32)[:, None, None]
    pos = rank + jnp.sum(jnp.where(idx[None] == expert, offs[:, None, None], 0), axis=0)
    n_tok_tiles = rows // TOKEN_TILE
    pos_tiles = pos.reshape(TOP_K, n_tok_tiles, TOKEN_TILE).transpose(1, 0, 2).reshape(n_tok_tiles, 1, TOP_K * TOKEN_TILE)
    n_slots = TOP_K * rows + N_EXPERTS * tm
    n_used = ends[-1] // tm
    tile_start = jnp.arange(n_slots // tm, dtype=I32) * tm
    tile_expert = jnp.sum((tile_start[:, None] >= ends[None, :]).astype(I32), axis=1)
    last_expert = jnp.sum((((n_used - 1) * tm) >= ends).astype(I32))
    tile_expert = jnp.where(tile_start < ends[-1], tile_expert, last_expert).astype(I32)
    return pos_tiles, counts, offs.astype(I32), tile_expert, n_used.astype(I32).reshape(1), n_slots


def _moe(h, g2, mods, layer, rows, seq, n_batch, w_router, e_bias, wg, wu, wd, sg, su, sd):
    xn, xw, idx, wts, rank, counts_f = _norm_mod(h, g2, mods, 3, 4, rows, seq, n_batch, router=(w_router, e_bias))
    hid_s = _swiglu(xn, sg, su, layer, rows)
    h = _mm_residual(hid_s, sd, layer, h, mods, 5, rows, seq, n_batch)
    pos_tiles, counts, offs, tile_expert, n_used, n_slots = _dispatch_plan(idx, rank, counts_f, rows)
    x_sorted = _dispatch(xw, pos_tiles, counts, offs, n_slots)
    y_sorted = _experts(x_sorted, tile_expert, n_used, wg, wu, wd, layer)
    return _combine(h, y_sorted, pos_tiles, wts.T, mods, 5, rows, seq, n_batch)


def _rope_tables(n_batch, seq, total_rows):
    pos = jnp.arange(seq)
    row = (pos // GRID_W).astype(F32)
    col = (pos % GRID_W).astype(F32)
    inv = ROPE_THETA ** (-jnp.arange(0, ROPE_AXIS_DIM, 2, dtype=F32) / ROPE_AXIS_DIM)
    ar = row[:, None] * inv
    ac = col[:, None] * inv
    cos = jnp.concatenate([jnp.cos(ar), jnp.cos(ar), jnp.cos(ac), jnp.cos(ac)], axis=-1)
    sin = jnp.concatenate([-jnp.sin(ar), jnp.sin(ar), -jnp.sin(ac), jnp.sin(ac)], axis=-1)
    n_ctx_rows = total_rows - n_batch * seq
    cos = jnp.concatenate([jnp.tile(cos, (n_batch, 1)), jnp.ones((n_ctx_rows, HEAD_DIM), F32)], axis=0)
    sin = jnp.concatenate([jnp.tile(sin, (n_batch, 1)), jnp.zeros((n_ctx_rows, HEAD_DIM), F32)], axis=0)
    return cos, sin


def kernel(x, c, ctx, c_ctx, ada_down, ada_up, ada_bias, norm1_g, norm2_g, swa_w_qkv, swa_w_o, swa_q_gain, swa_k_gain, swa_sink, fnet_w_o, nat_w_qkv, nat_w_o, nat_q_gain, nat_k_gain, nat_rpb, moe_router, moe_bias, moe_w_gate, moe_w_up, moe_w_down, shared_w_gate, shared_w_up, shared_w_down):
    n_batch, seq, _ = x.shape
    ctx_len = ctx.shape[1]
    lat_rows = n_batch * seq
    all_rows = lat_rows + n_batch * ctx_len
    assert n_batch + 1 <= COND_ROWS and seq % ROW_TILE == 0 and lat_rows % ROW_TILE == 0 and all_rows % ROW_TILE == 0
    assert seq // GRID_W >= NAT_KH and seq % NAT_BLOCK == 0 and ctx_len == NAT_BLOCK

    h = jnp.concatenate([x.reshape(lat_rows, D_MODEL), ctx.reshape(n_batch * ctx_len, D_MODEL)], axis=0)
    cond = jnp.concatenate([c, c_ctx[None, :], jnp.zeros((COND_ROWS - n_batch - 1, D_MODEL), F32)], axis=0)
    mods_all = _ada_mods(cond, ada_down, ada_up, ada_bias).reshape(DEPTH, COND_ROWS, N_MOD, 1, D_MODEL)
    cos, sin = _rope_tables(n_batch, seq, all_rows)
    ones_row = jnp.ones((N_HEADS * HEAD_DIM,), F32)

    slot = [0, 0, 0]
    for layer in range(DEPTH):
        kind = layer % 3
        s = slot[kind]
        slot[kind] += 1
        ctx_out = layer < DEPTH - 1
        rows = all_rows if ctx_out else lat_rows
        mods = mods_all[layer]
        common = dict(n_batch=n_batch, seq=seq, ctx_len=ctx_len, out_rows=rows)
        xn = _norm_mod(h, norm1_g[layer], mods, 0, 1, all_rows, seq, n_batch)
        if kind == 0:
            nq, nkv = N_HEADS * HEAD_DIM, N_KV_HEADS * HEAD_DIM
            gain = jnp.concatenate([jnp.tile(swa_q_gain[s], N_HEADS), jnp.tile(swa_k_gain[s], N_KV_HEADS),
                                    ones_row[:nkv]])[None, :]
            qkv = _qkv_proj(xn, swa_w_qkv, s, gain, cos, sin, nq + nkv, rope=True)
            att = dict(kind="swa", k_col0=N_HEADS, v_col0=N_HEADS + N_KV_HEADS, group=GQA_GROUP,
                       sink=swa_sink[s], **common)
            o = _attention(qkv, latent=True, **att)
            if ctx_out:
                o = _attention(qkv, latent=False, prev=o, **att)
            w_o = swa_w_o
        elif kind == 1:
            o = _fourier_mix(xn, all_rows, n_batch, seq, ctx_len, ctx_out)
            w_o = fnet_w_o
        else:
            width = N_HEADS * HEAD_DIM
            gain = jnp.concatenate([jnp.tile(nat_q_gain[s], N_HEADS), jnp.tile(nat_k_gain[s], N_HEADS),
                                    ones_row])[None, :]
            qkv = _qkv_proj(xn, nat_w_qkv, s, gain, cos, sin, 2 * width, rope=False)
            att = dict(kind="nat", k_col0=N_HEADS, v_col0=2 * N_HEADS, group=1, **common)
            o = _attention(qkv, latent=True, bias=_nat_bias_table(nat_rpb[s]), **att)
            if ctx_out:
                o = _attention(qkv, latent=False, prev=o, **att)
            w_o = nat_w_o
        h = _mm_residual(o, w_o, s, h, mods, 2, rows, seq, n_batch)
        h = _moe(h, norm2_g[layer], mods, layer, rows, seq, n_batch, moe_router[layer], moe_bias[layer],
                 moe_w_gate, moe_w_up, moe_w_down, shared_w_gate, shared_w_up, shared_w_down)
    return h[:lat_rows].reshape(n_batch, seq, D_MODEL)
```

```python
import functools

import numpy as np
import jax
import jax.numpy as jnp
from jax import lax
from jax.experimental import pallas as pl
from jax.experimental.pallas import tpu as pltpu

F32 = jnp.float32
BF16 = jnp.bfloat16
U32 = jnp.uint32
I32 = jnp.int32

D_MODEL = 4096
DEPTH = 4
GRID_W = 64
HEAD_DIM = 128
N_HEADS = 32
N_KV_HEADS = 8
GQA_GROUP = N_HEADS // N_KV_HEADS
WINDOW = 128
ROPE_THETA = 10000.0
ROPE_AXIS_DIM = HEAD_DIM // 2
NAT_KH = 8
NAT_KW = 16
NAT_ROWS_PER_BLOCK = 4
NAT_BLOCK = NAT_ROWS_PER_BLOCK * GRID_W
FOURIER_GROUPS = 8
FOURIER_GROUP_DIM = D_MODEL // FOURIER_GROUPS
N_EXPERTS = 32
TOP_K = 4
N_EXPERT_GROUPS = 4
GROUP_SIZE = N_EXPERTS // N_EXPERT_GROUPS
TOPK_GROUPS = 2
EXPERT_HIDDEN = 256
SHARED_HIDDEN = 1024
ROUTED_SCALE = 2.5
ADA_RANK = 256
N_MOD = 6
EPS = 1e-6
MASKED = -1e30

VMEM_LIMIT_BYTES = 56 * 1024 * 1024
ROW_TILE = 1024
COL_TILE = 512
EXPERT_TILE = 256
TOKEN_TILE = 256
HALF_D = D_MODEL // 2
LANES = 128
SLAB = HALF_D // LANES
COND_ROWS = 16
DMA_UNROLL = 8
ATTN_HEADS_PER_STEP = 4


def _params(*sem, **kw):
    return pltpu.CompilerParams(dimension_semantics=sem, vmem_limit_bytes=VMEM_LIMIT_BYTES, **kw)


def _dot(a, b):
    return jnp.dot(a, b, preferred_element_type=F32)


def _dot_nt(a, b):
    return lax.dot_general(a, b, (((1,), (1,)), ((), ())), preferred_element_type=F32)


def _silu(x):
    return x * (1.0 / (1.0 + jnp.exp(-x)))


def _pack_halves(lo, hi):
    lo_bits = lax.bitcast_convert_type(lo.astype(BF16).astype(F32), U32)
    hi_bits = lax.bitcast_convert_type(hi.astype(BF16).astype(F32), U32)
    return (lo_bits >> 16) | hi_bits


def _unpack_halves(word):
    lo = lax.bitcast_convert_type(word << 16, F32)
    hi = lax.bitcast_convert_type(word & jnp.uint32(0xFFFF0000), F32)
    return lo, hi


def _slab_rows(first_token, n_tokens, part):
    return pl.ds(first_token * SLAB + part, n_tokens, stride=SLAB)


def _store_slabs(ref, packed):
    n = packed.shape[0]
    for part in range(SLAB):
        ref[_slab_rows(0, n, part), :] = packed[:, part * LANES:(part + 1) * LANES]


def _load_slabs(ref, n):
    return jnp.concatenate([ref[_slab_rows(0, n, part), :] for part in range(SLAB)], axis=1)


def _ada_kernel(cond_ref, down_ref, up_ref, bias_ref, o_ref):
    c = cond_ref[...]
    z = _dot(_silu(c).astype(BF16), down_ref[0].astype(BF16))
    o_ref[0] = _dot(z.astype(BF16), up_ref[0].astype(BF16)) + bias_ref[0]


def _ada_mods(cond, ada_down, ada_up, ada_bias):
    return pl.pallas_call(
        _ada_kernel,
        grid=(DEPTH, N_MOD),
        in_specs=[
            pl.BlockSpec((COND_ROWS, D_MODEL), lambda l, j: (0, 0)),
            pl.BlockSpec((1, D_MODEL, ADA_RANK), lambda l, j: (l, 0, 0)),
            pl.BlockSpec((1, ADA_RANK, D_MODEL), lambda l, j: (l, 0, j)),
            pl.BlockSpec((1, 1, D_MODEL), lambda l, j: (l, 0, j)),
        ],
        out_specs=pl.BlockSpec((1, COND_ROWS, D_MODEL), lambda l, j: (l, 0, j)),
        out_shape=jax.ShapeDtypeStruct((DEPTH, COND_ROWS, N_MOD * D_MODEL), F32),
        compiler_params=_params("arbitrary", "arbitrary"),
        name="ada_mods",
    )(cond, ada_down, ada_up, ada_bias[:, None, :])


def _norm_mod_tile(h_ref, g_ref, shift_ref, scale_ref):
    x = h_ref[...]
    ms = jnp.mean(x * x, axis=-1, keepdims=True)
    y = x * lax.rsqrt(ms + EPS) * g_ref[...]
    return y * (1.0 + scale_ref[0, 0]) + shift_ref[0, 0]


def _norm_mod_kernel(h_ref, g_ref, shift_ref, scale_ref, o_ref):
    o_ref[...] = _norm_mod_tile(h_ref, g_ref, shift_ref, scale_ref).astype(BF16)


def _split_bf16(x):
    hi = x.astype(BF16)
    lo = (x - hi.astype(F32)).astype(BF16)
    return hi, lo


def _first_index_of_max(v, idx, n):
    m = jnp.max(v, axis=0, keepdims=True)
    i = jnp.min(jnp.where(v == m, idx, n), axis=0, keepdims=True)
    return m, i


def _norm_route_kernel(h_ref, g_ref, shift_ref, scale_ref, wr_ref, eb_ref,
                       o_ref, xw_ref, idx_ref, wts_ref, rank_ref, cnt_ref, run_ref):
    @pl.when(pl.program_id(0) == 0)
    def _():
        run_ref[...] = jnp.zeros_like(run_ref)

    xn = _norm_mod_tile(h_ref, g_ref, shift_ref, scale_ref)
    o_ref[...] = xn.astype(BF16)
    _store_slabs(xw_ref, _pack_halves(xn[:, :HALF_D], xn[:, HALF_D:]))
    x_hi, x_lo = _split_bf16(xn)
    w_hi, w_lo = _split_bf16(wr_ref[...])
    logits = _dot(x_hi, w_hi) + _dot(x_lo, w_hi) + _dot(x_hi, w_lo)
    lt = logits.T[:N_EXPERTS]
    scores = 1.0 / (1.0 + jnp.exp(-lt))
    sel = scores + eb_ref[...]
    n_tok = sel.shape[1]
    sub = lax.broadcasted_iota(I32, (GROUP_SIZE, n_tok), 0).astype(F32)
    grp_scores = []
    for g in range(N_EXPERT_GROUPS):
        sg = sel[g * GROUP_SIZE:(g + 1) * GROUP_SIZE]
        m1, i1 = _first_index_of_max(sg, sub, float(GROUP_SIZE))
        m2 = jnp.max(jnp.where(sub == i1, -jnp.inf, sg), axis=0, keepdims=True)
        grp_scores.append(m1 + m2)
    eidx_i = lax.broadcasted_iota(I32, sel.shape, 0)
    eidx = eidx_i.astype(F32)
    egrp = (eidx_i // GROUP_SIZE).astype(F32)
    best = jnp.full((1, n_tok), -jnp.inf, F32)
    for g in range(N_EXPERT_GROUPS):
        best = jnp.maximum(best, grp_scores[g])
    g1 = jnp.full((1, n_tok), float(N_EXPERT_GROUPS), F32)
    for g in reversed(range(N_EXPERT_GROUPS)):
        g1 = jnp.where(grp_scores[g] == best, float(g), g1)
    second = jnp.full((1, n_tok), -jnp.inf, F32)
    for g in range(N_EXPERT_GROUPS):
        second = jnp.maximum(second, jnp.where(g1 == float(g), -jnp.inf, grp_scores[g]))
    g2 = jnp.full((1, n_tok), float(N_EXPERT_GROUPS), F32)
    for g in reversed(range(N_EXPERT_GROUPS)):
        g2 = jnp.where((grp_scores[g] == second) & (g1 != float(g)), float(g), g2)
    masked = jnp.where((egrp == g1) | (egrp == g2), sel, -jnp.inf)
    picks, weights, hits = [], [], []
    for _ in range(TOP_K):
        _, ik = _first_index_of_max(masked, eidx, float(N_EXPERTS))
        hit = eidx == ik
        weights.append(jnp.sum(jnp.where(hit, scores, 0.0), axis=0, keepdims=True))
        picks.append(ik)
        hits.append(hit)
        masked = jnp.where(hit, -jnp.inf, masked)
    w = jnp.concatenate(weights, axis=0)
    idx_ref[...] = jnp.concatenate(picks, axis=0).astype(I32)
    wts_ref[...] = w / jnp.sum(w, axis=0, keepdims=True) * ROUTED_SCALE
    member = jnp.zeros(sel.shape, F32)
    for hit in hits:
        member = jnp.where(hit, 1.0, member)
    earlier = lax.broadcasted_iota(I32, (n_tok, n_tok), 0) < lax.broadcasted_iota(I32, (n_tok, n_tok), 1)
    before = run_ref[...] + _dot(member.astype(BF16), jnp.where(earlier, 1.0, 0.0).astype(BF16))
    rank_ref[...] = jnp.concatenate(
        [jnp.sum(jnp.where(hit, before, 0.0), axis=0, keepdims=True) for hit in hits], axis=0).astype(I32)
    run_ref[...] = run_ref[...] + jnp.sum(member, axis=1, keepdims=True)
    cnt_ref[...] = run_ref[...]


def _seg_of_tile(i, tm, seq, n_batch):
    return jnp.minimum((i * tm) // seq, n_batch)


def _norm_mod(h, g, mods, shift_k, scale_k, rows, seq, n_batch, router=None):
    tm = TOKEN_TILE
    seg = functools.partial(_seg_of_tile, tm=tm, seq=seq, n_batch=n_batch)
    in_specs = [
        pl.BlockSpec((tm, D_MODEL), lambda i: (i, 0)),
        pl.BlockSpec((1, D_MODEL), lambda i: (0, 0)),
        pl.BlockSpec((1, 1, 1, D_MODEL), lambda i: (seg(i), shift_k, 0, 0)),
        pl.BlockSpec((1, 1, 1, D_MODEL), lambda i: (seg(i), scale_k, 0, 0)),
    ]
    xn_spec = pl.BlockSpec((tm, D_MODEL), lambda i: (i, 0))
    xn_shape = jax.ShapeDtypeStruct((rows, D_MODEL), BF16)
    if router is None:
        return pl.pallas_call(
            _norm_mod_kernel, grid=(rows // tm,), in_specs=in_specs, out_specs=xn_spec, out_shape=xn_shape,
            compiler_params=_params("arbitrary"), name="norm_mod",
        )(h, g.reshape(1, D_MODEL), mods, mods)
    w_router, e_bias = router
    w_pad = jnp.pad(w_router, ((0, 0), (0, 128 - N_EXPERTS)))
    in_specs += [
        pl.BlockSpec((D_MODEL, 128), lambda i: (0, 0)),
        pl.BlockSpec((N_EXPERTS, 1), lambda i: (0, 0)),
    ]
    per_tok = pl.BlockSpec((TOP_K, tm), lambda i: (0, i))
    return pl.pallas_call(
        _norm_route_kernel, grid=(rows // tm,), in_specs=in_specs,
        out_specs=[xn_spec, pl.BlockSpec((tm * SLAB, LANES), lambda i: (i, 0)), per_tok, per_tok, per_tok,
                   pl.BlockSpec((N_EXPERTS, 1), lambda i: (0, 0))],
        out_shape=[xn_shape, jax.ShapeDtypeStruct((rows * SLAB, LANES), U32),
                   jax.ShapeDtypeStruct((TOP_K, rows), I32), jax.ShapeDtypeStruct((TOP_K, rows), F32),
                   jax.ShapeDtypeStruct((TOP_K, rows), I32), jax.ShapeDtypeStruct((N_EXPERTS, 1), F32)],
        scratch_shapes=[pltpu.VMEM((N_EXPERTS, 1), F32)],
        compiler_params=_params("arbitrary"), name="norm_route",
    )(h, g.reshape(1, D_MODEL), mods, mods, w_pad, e_bias.reshape(N_EXPERTS, 1))


def _cast_weight_once(w_ref, w_scr):
    @pl.when(pl.program_id(1) == 0)
    def _():
        w_scr[...] = w_ref[0].astype(BF16)


def _qkv_kernel(a_ref, w_ref, gain_ref, cos_ref, sin_ref, o_ref, w_scr, *, n_qk_tiles, rope):
    _cast_weight_once(w_ref, w_scr)
    j = pl.program_id(0)
    acc = _dot(a_ref[...], w_scr[...])
    n_heads = acc.shape[1] // HEAD_DIM

    @pl.when(j < n_qk_tiles)
    def _():
        lane = lax.broadcasted_iota(I32, (acc.shape[0], HEAD_DIM), 1)
        quarter = ROPE_AXIS_DIM // 2
        first_half = (lane % ROPE_AXIS_DIM) < quarter
        outs = []
        for hh in range(n_heads):
            x = acc[:, hh * HEAD_DIM:(hh + 1) * HEAD_DIM]
            ms = jnp.mean(x * x, axis=-1, keepdims=True)
            y = x * lax.rsqrt(ms + EPS) * gain_ref[:, hh * HEAD_DIM:(hh + 1) * HEAD_DIM]
            if rope:
                partner = jnp.where(first_half, pltpu.roll(y, HEAD_DIM - quarter, 1), pltpu.roll(y, quarter, 1))
                y = y * cos_ref[...] + partner * sin_ref[...]
            outs.append(y.astype(BF16))
        o_ref[...] = jnp.concatenate(outs, axis=1)

    @pl.when(j >= n_qk_tiles)
    def _():
        o_ref[...] = acc.astype(BF16)


def _qkv_proj(xn, w_stack, w_slot, gain_row, cos, sin, n_qk_cols, rope):
    rows, width = xn.shape[0], w_stack.shape[2]
    tm, tn = ROW_TILE, COL_TILE
    return pl.pallas_call(
        functools.partial(_qkv_kernel, n_qk_tiles=n_qk_cols // tn, rope=rope),
        grid=(width // tn, rows // tm),
        in_specs=[
            pl.BlockSpec((tm, D_MODEL), lambda j, i: (i, 0)),
            pl.BlockSpec((1, D_MODEL, tn), lambda j, i: (w_slot, 0, j)),
            pl.BlockSpec((1, tn), lambda j, i: (0, j)),
            pl.BlockSpec((tm, HEAD_DIM), lambda j, i: (i, 0)),
            pl.BlockSpec((tm, HEAD_DIM), lambda j, i: (i, 0)),
        ],
        out_specs=pl.BlockSpec((tm, tn), lambda j, i: (i, j)),
        out_shape=jax.ShapeDtypeStruct((rows, width), BF16),
        scratch_shapes=[pltpu.VMEM((D_MODEL, tn), BF16)],
        compiler_params=_params("arbitrary", "arbitrary"),
        name="qkv_proj",
    )(xn, w_stack, gain_row, cos, sin)


def _mm_res_kernel(a_ref, w_ref, h_ref, gate_ref, o_ref, w_scr):
    _cast_weight_once(w_ref, w_scr)
    o_ref[...] = h_ref[...] + gate_ref[0, 0] * _dot(a_ref[...], w_scr[...])


def _mm_residual(a, w_stack, w_slot, h, mods, gate_k, rows, seq, n_batch):
    _, k_dim, width = w_stack.shape
    tm, tn = ROW_TILE, COL_TILE
    seg = functools.partial(_seg_of_tile, tm=tm, seq=seq, n_batch=n_batch)
    return pl.pallas_call(
        _mm_res_kernel,
        grid=(width // tn, rows // tm),
        in_specs=[
            pl.BlockSpec((tm, k_dim), lambda j, i: (i, 0)),
            pl.BlockSpec((1, k_dim, tn), lambda j, i: (w_slot, 0, j)),
            pl.BlockSpec((tm, tn), lambda j, i: (i, j)),
            pl.BlockSpec((1, 1, 1, tn), lambda j, i: (seg(i), gate_k, 0, j)),
        ],
        out_specs=pl.BlockSpec((tm, tn), lambda j, i: (i, j)),
        out_shape=jax.ShapeDtypeStruct((rows, width), F32),
        scratch_shapes=[pltpu.VMEM((k_dim, tn), BF16)],
        compiler_params=_params("arbitrary", "arbitrary"),
        name="mm_residual",
    )(a, w_stack, h, mods)


def _swiglu_kernel(a_ref, wg_ref, wu_ref, o_ref, wg_scr, wu_scr):
    _cast_weight_once(wg_ref, wg_scr)
    _cast_weight_once(wu_ref, wu_scr)
    a = a_ref[...]
    o_ref[...] = (_silu(_dot(a, wg_scr[...])) * _dot(a, wu_scr[...])).astype(BF16)


def _swiglu(a, wg_stack, wu_stack, layer, rows):
    width = wg_stack.shape[2]
    tm, tn = ROW_TILE, COL_TILE // 2
    w_spec = pl.BlockSpec((1, D_MODEL, tn), lambda j, i: (layer, 0, j))
    return pl.pallas_call(
        _swiglu_kernel,
        grid=(width // tn, rows // tm),
        in_specs=[pl.BlockSpec((tm, D_MODEL), lambda j, i: (i, 0)), w_spec, w_spec],
        out_specs=pl.BlockSpec((tm, tn), lambda j, i: (i, j)),
        out_shape=jax.ShapeDtypeStruct((rows, width), BF16),
        scratch_shapes=[pltpu.VMEM((D_MODEL, tn), BF16), pltpu.VMEM((D_MODEL, tn), BF16)],
        compiler_params=_params("arbitrary", "arbitrary"),
        name="shared_swiglu",
    )(a, wg_stack, wu_stack)


def _attn_kernel(*refs, n_local, group, block, kind, has_sink, has_prev, n_blocks, heads):
    refs = list(refs)
    sink_ref = refs.pop(0) if has_sink else None
    q_ref = refs.pop(0)
    k_loc = [refs.pop(0) for _ in range(n_local)]
    v_loc = [refs.pop(0) for _ in range(n_local)]
    kc_ref, vc_ref = refs.pop(0), refs.pop(0)
    bias_ref = refs.pop(0) if kind == "nat" else None
    if has_prev:
        refs.pop(0)
    o_ref = refs.pop(0)
    head0 = pl.program_id(1) * heads
    i = pl.program_id(2)
    scale = HEAD_DIM ** -0.5
    n_rows = group * block

    valid = None
    if n_local:
        row = lax.broadcasted_iota(I32, (n_rows, 1), 0)
        col = lax.broadcasted_iota(I32, (1, n_local * block), 1)
        if kind == "swa":
            rel = col - block - (row % block)
            lo = jnp.where(i == 0, block, 0)
            hi = jnp.where(i == n_blocks - 1, 2 * block, 3 * block)
            valid = (rel >= -WINDOW) & (rel <= WINDOW) & (col >= lo) & (col < hi)
        else:
            q_row = NAT_ROWS_PER_BLOCK * i + row // GRID_W
            k_row = NAT_ROWS_PER_BLOCK * (i - 1) + col // GRID_W
            r0 = jnp.clip(q_row - NAT_KH // 2, 0, n_blocks * NAT_ROWS_PER_BLOCK - NAT_KH)
            valid = (k_row >= r0) & (k_row < r0 + NAT_KH)
    g_of_row = lax.broadcasted_iota(I32, (n_rows, 1), 0) // block if has_sink else None

    outs = []
    for hd in range(heads):
        kv_cols = slice(hd * HEAD_DIM, (hd + 1) * HEAD_DIM)
        if group > 1:
            q = jnp.concatenate([q_ref[:, (hd * group + g) * HEAD_DIM:(hd * group + g + 1) * HEAD_DIM]
                                 for g in range(group)], axis=0)
        else:
            q = q_ref[:, kv_cols]
        s_ctx = _dot_nt(q, kc_ref[:, kv_cols]) * scale
        m = jnp.max(s_ctx, axis=-1, keepdims=True)
        if n_local:
            k = jnp.concatenate([r[:, kv_cols] for r in k_loc], axis=0)
            v = jnp.concatenate([r[:, kv_cols] for r in v_loc], axis=0)
            s_loc = _dot_nt(q, k) * scale
            if kind == "nat":
                s_loc = s_loc + bias_ref[hd]
            s_loc = jnp.where(valid, s_loc, MASKED)
            m = jnp.maximum(m, jnp.max(s_loc, axis=-1, keepdims=True))
        if has_sink:
            sink = jnp.zeros((n_rows, 1), F32)
            for g in range(group):
                sink = jnp.where(g_of_row == g, sink_ref[(head0 + hd) * group + g], sink)
            m = jnp.maximum(m, sink)
        p_ctx = jnp.exp(s_ctx - m)
        denom = jnp.sum(p_ctx, axis=-1, keepdims=True)
        o = _dot(p_ctx.astype(BF16), vc_ref[:, kv_cols])
        if n_local:
            p_loc = jnp.exp(s_loc - m)
            denom = denom + jnp.sum(p_loc, axis=-1, keepdims=True)
            o = o + _dot(p_loc.astype(BF16), v)
        if has_sink:
            denom = denom + jnp.exp(sink - m)
        o = (o * (1.0 / denom)).astype(BF16)
        outs += [o[g * block:(g + 1) * block] for g in range(group)]
    o_ref[...] = jnp.concatenate(outs, axis=1) if len(outs) > 1 else outs[0]


def _attention(qkv, *, kind, latent, n_batch, seq, ctx_len, k_col0, v_col0, group, out_rows,
               sink=None, bias=None, prev=None):
    heads = ATTN_HEADS_PER_STEP
    n_kv = v_col0 - k_col0
    assert n_kv % heads == 0 and k_col0 % heads == 0 and v_col0 % heads == 0
    n_steps_h = n_kv // heads
    kc0, vc0 = k_col0 // heads, v_col0 // heads
    block = (WINDOW if kind == "swa" else NAT_BLOCK) if latent else ctx_len
    n_blocks = seq // block if latent else 1
    n_local = 3 if latent else 0
    ctx_blk0 = n_batch * seq // ctx_len
    q_blk0 = 0 if latent else n_batch * seq // block
    has_sink = sink is not None
    nat_order = kind == "nat"

    def ids(a, b, c):
        return (b, a, c) if nat_order else (a, b, c)

    def q_map(a, b, c):
        bb, hh, ii = ids(a, b, c)
        return (q_blk0 + bb * n_blocks + ii, hh)

    def loc_map(delta, col0):
        def f(a, b, c):
            bb, hh, ii = ids(a, b, c)
            return (bb * n_blocks + jnp.clip(ii + delta, 0, n_blocks - 1), col0 + hh)
        return f

    def ctx_map(col0):
        def f(a, b, c):
            bb, hh, _ = ids(a, b, c)
            return (ctx_blk0 + bb, col0 + hh)
        return f

    in_specs, args = [], []
    if has_sink:
        in_specs.append(pl.BlockSpec(memory_space=pltpu.SMEM))
        args.append(sink.astype(F32))
    in_specs.append(pl.BlockSpec((block, heads * group * HEAD_DIM), q_map))
    args.append(qkv)
    for col0 in ((kc0, vc0) if latent else ()):
        for delta in (-1, 0, 1):
            in_specs.append(pl.BlockSpec((block, heads * HEAD_DIM), loc_map(delta, col0)))
            args.append(qkv)
    for col0 in (kc0, vc0):
        in_specs.append(pl.BlockSpec((ctx_len, heads * HEAD_DIM), ctx_map(col0)))
        args.append(qkv)
    use_bias = nat_order and latent
    if use_bias:
        in_specs.append(pl.BlockSpec((heads, block, 3 * block), lambda a, b, c: (a, 0, 0)))
        args.append(bias)
    aliases = {}
    if prev is not None:
        aliases = {len(args): 0}
        in_specs.append(pl.BlockSpec(memory_space=pl.ANY))
        args.append(prev)
    grid = (n_steps_h, n_batch, n_blocks) if nat_order else (n_batch, n_steps_h, n_blocks)
    kern = functools.partial(_attn_kernel, n_local=n_local, group=group, block=block,
                             kind=("nat" if use_bias else kind if latent else "ctx"),
                             has_sink=has_sink, has_prev=prev is not None, n_blocks=n_blocks, heads=heads)
    return pl.pallas_call(
        kern, grid=grid, in_specs=in_specs,
        out_specs=pl.BlockSpec((block, heads * group * HEAD_DIM), q_map),
        out_shape=jax.ShapeDtypeStruct((out_rows, n_kv * group * HEAD_DIM), BF16),
        input_output_aliases=aliases,
        compiler_params=_params("arbitrary", "arbitrary", "arbitrary"),
        name=f"attn_{kind}_{'latent' if latent else 'ctx'}",
    )(*args)


def _nat_bias_table(rpb):
    q_c = np.arange(GRID_W)[:, None]
    k_c = np.arange(GRID_W)[None, :]
    d_col = np.clip(k_c - q_c + NAT_KW - 1, 0, 2 * NAT_KW - 2)
    c0 = np.clip(q_c - NAT_KW // 2, 0, GRID_W - NAT_KW)
    col_valid = (k_c >= c0) & (k_c < c0 + NAT_KW)
    onehot = (np.arange(2 * NAT_KW - 1)[:, None, None] == d_col[None]).astype(np.float32)
    by_col = jnp.einsum("hrd,dqk->hrqk", rpb.astype(F32), jnp.asarray(onehot), precision=lax.Precision.HIGHEST)
    by_col = jnp.where(col_valid[None, None], by_col, MASKED)
    n_key_rows = 3 * NAT_ROWS_PER_BLOCK
    rows = []
    for q_r in range(NAT_ROWS_PER_BLOCK):
        d_row = [k_r - NAT_ROWS_PER_BLOCK - q_r + NAT_KH - 1 for k_r in range(n_key_rows)]
        rows.append(jnp.concatenate([by_col[:, d] for d in d_row], axis=-1))
    return jnp.concatenate(rows, axis=1)


def _dft_chan_kernel(a_ref, w_ref, o_ref):
    o_ref[0] = _dot(a_ref[...], w_ref[0]).astype(BF16)


def _dft_pos_kernel(cn_ref, sn_ref, xc_ref, xs_ref, *rest, scale):
    o_ref = rest[-1]
    o_ref[...] = ((_dot(cn_ref[...], xc_ref[0]) - _dot(sn_ref[...], xs_ref[0])) * scale).astype(BF16)


def _dft_tables(n):
    k = jnp.arange(n, dtype=I32)
    ang = ((k[:, None] * k[None, :]) % n).astype(F32) * (2.0 * np.pi / n)
    return jnp.cos(ang).astype(BF16), jnp.sin(ang).astype(BF16)


def _fourier_mix(xn, rows, n_batch, seq, ctx_len, ctx_out):
    gd = FOURIER_GROUP_DIM
    cd, sd = _dft_tables(gd)
    chan = jnp.stack([cd, sd])
    tm = ROW_TILE
    xcs = pl.pallas_call(
        _dft_chan_kernel,
        grid=(rows // tm, FOURIER_GROUPS, 2),
        in_specs=[pl.BlockSpec((tm, gd), lambda i, g, t: (i, g)),
                  pl.BlockSpec((1, gd, gd), lambda i, g, t: (t, 0, 0))],
        out_specs=pl.BlockSpec((1, tm, gd), lambda i, g, t: (t, i, g)),
        out_shape=jax.ShapeDtypeStruct((2, rows, D_MODEL), BF16),
        compiler_params=_params("arbitrary", "arbitrary", "arbitrary"),
        name="dft_channels",
    )(xn, chan)

    def pos_call(n, tmo, row_blk0, in_blk0, out_rows, prev):
        cn, sn = _dft_tables(n)
        tn = COL_TILE
        per = n // tmo
        in_specs = [
            pl.BlockSpec((tmo, n), lambda b, i, j: (i, 0)),
            pl.BlockSpec((tmo, n), lambda b, i, j: (i, 0)),
            pl.BlockSpec((1, n, tn), lambda b, i, j: (0, in_blk0 + b, j)),
            pl.BlockSpec((1, n, tn), lambda b, i, j: (1, in_blk0 + b, j)),
        ]
        args = [cn, sn, xcs, xcs]
        aliases = {}
        if prev is not None:
            in_specs.append(pl.BlockSpec(memory_space=pl.ANY))
            args.append(prev)
            aliases = {4: 0}
        return pl.pallas_call(
            functools.partial(_dft_pos_kernel, scale=float((n * gd) ** -0.5)),
            grid=(n_batch, per, D_MODEL // tn),
            in_specs=in_specs,
            out_specs=pl.BlockSpec((tmo, tn), lambda b, i, j: (row_blk0 + b * per + i, j)),
            out_shape=jax.ShapeDtypeStruct((out_rows, D_MODEL), BF16),
            input_output_aliases=aliases,
            compiler_params=_params("arbitrary", "arbitrary", "arbitrary"),
            name=f"dft_positions_{n}",
        )(*args)

    z = pos_call(seq, 512, 0, 0, rows, None)
    if ctx_out:
        z = pos_call(ctx_len, ctx_len, n_batch * seq // ctx_len, n_batch * seq // ctx_len, rows, z)
    return z


def _slab_copy(src_ref, src_token, dst_ref, dst_token, sem):
    src = src_ref.at[pl.ds(pl.multiple_of(src_token * SLAB, SLAB), SLAB)]
    dst = dst_ref.at[pl.ds(pl.multiple_of(dst_token * SLAB, SLAB), SLAB)]
    return pltpu.make_async_copy(src, dst, sem)


def _load_slots(pos_ref, pos_smem, sem):
    cp = pltpu.make_async_copy(pos_ref.at[0, 0], pos_smem, sem)
    cp.start()
    cp.wait()


def _for_token_groups(n_tokens, fn):
    def group(g, c):
        for k in range(TOP_K):
            for u in range(DMA_UNROLL):
                fn(k, g * DMA_UNROLL + u)
        return c
    lax.fori_loop(0, n_tokens // DMA_UNROLL, group, 0)


def _dispatch_kernel(cnt_ref, off_ref, pos_ref, xw_ref, xs_ref, pos_smem, zero_ref, sem_idx, sem_rows):
    i = pl.program_id(0)
    n_steps = pl.num_programs(0)
    tm = pos_smem.shape[0] // TOP_K

    @pl.when(i == 0)
    def _():
        zero_ref[...] = jnp.zeros_like(zero_ref)

        def per_expert(e, carry):
            count = cnt_ref[e]
            first = off_ref[e] + count
            n_pad = (count + EXPERT_TILE - 1) // EXPERT_TILE * EXPERT_TILE - count

            def start(r, c):
                _slab_copy(zero_ref, 0, xs_ref, first + r, sem_rows.at[0]).start()
                return c

            def wait(r, c):
                _slab_copy(zero_ref, 0, xs_ref, first + r, sem_rows.at[0]).wait()
                return c

            lax.fori_loop(0, n_pad, start, 0)
            lax.fori_loop(0, n_pad, wait, 0)
            return carry

        lax.fori_loop(0, N_EXPERTS, per_expert, 0)

    _load_slots(pos_ref, pos_smem, sem_idx)
    base = i * tm
    sem_now = sem_rows.at[i % 2]
    _for_token_groups(tm, lambda k, t: _slab_copy(xw_ref, base + t, xs_ref, pos_smem[k * tm + t], sem_now).start())

    def drain(sem):
        _for_token_groups(tm, lambda k, t: _slab_copy(xw_ref, 0, xs_ref, 0, sem).wait())

    @pl.when(i > 0)
    def _():
        drain(sem_rows.at[(i + 1) % 2])

    @pl.when(i == n_steps - 1)
    def _():
        drain(sem_now)


def _dispatch(xw, pos_tiles, counts, offs, n_slots):
    rows = xw.shape[0] // SLAB
    tm = TOKEN_TILE
    grid_spec = pltpu.PrefetchScalarGridSpec(
        num_scalar_prefetch=2,
        grid=(rows // tm,),
        in_specs=[
            pl.BlockSpec((1, 1, TOP_K * tm), lambda i, cnt, off: (i, 0, 0)),
            pl.BlockSpec(memory_space=pl.ANY),
        ],
        out_specs=pl.BlockSpec(memory_space=pl.ANY),
        scratch_shapes=[
            pltpu.SMEM((TOP_K * tm,), I32),
            pltpu.VMEM((SLAB, LANES), U32),
            pltpu.SemaphoreType.DMA,
            pltpu.SemaphoreType.DMA((2,)),
        ],
    )
    return pl.pallas_call(
        _dispatch_kernel, grid_spec=grid_spec,
        out_shape=jax.ShapeDtypeStruct((n_slots * SLAB, LANES), U32),
        compiler_params=_params("arbitrary", disable_bounds_checks=True),
        name="moe_dispatch",
    )(counts, offs, pos_tiles, xw)


def _expert_kernel(te_ref, nu_ref, x_ref, wg_ref, wu_ref, wd_ref, o_ref, wg_scr, wu_scr, wd_scr):
    j = pl.program_id(0)

    @pl.when(j < nu_ref[0])
    def _():
        @pl.when((j == 0) | (te_ref[j] != te_ref[jnp.maximum(j - 1, 0)]))
        def _():
            wg_scr[...] = wg_ref[0, 0].astype(BF16)
            wu_scr[...] = wu_ref[0, 0].astype(BF16)
            wd_scr[...] = wd_ref[0, 0].astype(BF16)

        x_lo, x_hi = _unpack_halves(_load_slabs(x_ref, EXPERT_TILE))
        x_lo, x_hi = x_lo.astype(BF16), x_hi.astype(BF16)
        gate = _dot(x_lo, wg_scr[:HALF_D]) + _dot(x_hi, wg_scr[HALF_D:])
        up = _dot(x_lo, wu_scr[:HALF_D]) + _dot(x_hi, wu_scr[HALF_D:])
        hid = (_silu(gate) * up).astype(BF16)
        _store_slabs(o_ref, _pack_halves(_dot(hid, wd_scr[:, :HALF_D]), _dot(hid, wd_scr[:, HALF_D:])))


def _experts(x_sorted, tile_expert, n_used, wg, wu, wd, layer):
    n_rows = x_sorted.shape[0] // SLAB
    tm = EXPERT_TILE

    def row_map(j, te, nu):
        return (jnp.minimum(j, nu[0] - 1), 0)

    def w_map(j, te, nu):
        return (layer, te[j], 0, 0)

    grid_spec = pltpu.PrefetchScalarGridSpec(
        num_scalar_prefetch=2,
        grid=(n_rows // tm,),
        in_specs=[
            pl.BlockSpec((tm * SLAB, LANES), row_map),
            pl.BlockSpec((1, 1, D_MODEL, EXPERT_HIDDEN), w_map),
            pl.BlockSpec((1, 1, D_MODEL, EXPERT_HIDDEN), w_map),
            pl.BlockSpec((1, 1, EXPERT_HIDDEN, D_MODEL), w_map),
        ],
        out_specs=pl.BlockSpec((tm * SLAB, LANES), row_map),
        scratch_shapes=[
            pltpu.VMEM((D_MODEL, EXPERT_HIDDEN), BF16),
            pltpu.VMEM((D_MODEL, EXPERT_HIDDEN), BF16),
            pltpu.VMEM((EXPERT_HIDDEN, D_MODEL), BF16),
        ],
    )
    return pl.pallas_call(
        _expert_kernel, grid_spec=grid_spec,
        out_shape=jax.ShapeDtypeStruct((n_rows * SLAB, LANES), U32),
        compiler_params=_params("arbitrary"),
        name="routed_experts",
    )(tile_expert, n_used, x_sorted, wg, wu, wd)


def _combine_kernel(pos_ref, nxt_ref, h_ref, w_ref, gate_ref, y_ref, o_ref, pos_smem, buf_ref, sem_idx, sem_rows):
    i = pl.program_id(0)
    n_steps = pl.num_programs(0)
    tm = h_ref.shape[0]
    cur = i % 2

    def gather(slot_ref, which):
        _load_slots(slot_ref, pos_smem, sem_idx)
        _for_token_groups(tm, lambda k, t: _slab_copy(
            y_ref, pos_smem[k * tm + t], buf_ref.at[which, k], t, sem_rows.at[which]).start())

    @pl.when(i == 0)
    def _():
        gather(pos_ref, 0)

    @pl.when(i + 1 < n_steps)
    def _():
        gather(nxt_ref, 1 - cur)

    _for_token_groups(tm, lambda k, t: _slab_copy(y_ref, 0, buf_ref.at[cur, k], t, sem_rows.at[cur]).wait())
    sub = 8

    def rows_chunk(r, c):
        first = pl.multiple_of(r * sub, sub)
        rs = pl.ds(first, sub)
        w = w_ref[rs, :]
        wk = [jnp.broadcast_to(w[:, k:k + 1], (sub, LANES)) for k in range(TOP_K)]
        for part in range(SLAB):
            acc_lo = jnp.zeros((sub, LANES), F32)
            acc_hi = jnp.zeros((sub, LANES), F32)
            for k in range(TOP_K):
                y_lo, y_hi = _unpack_halves(buf_ref[cur, k, _slab_rows(first, sub, part), :])
                acc_lo = acc_lo + wk[k] * y_lo
                acc_hi = acc_hi + wk[k] * y_hi
            lo = pl.ds(part * LANES, LANES)
            hi = pl.ds(HALF_D + part * LANES, LANES)
            o_ref[rs, lo] = h_ref[rs, lo] + gate_ref[0, 0, :, lo] * acc_lo
            o_ref[rs, hi] = h_ref[rs, hi] + gate_ref[0, 0, :, hi] * acc_hi
        return c

    lax.fori_loop(0, tm // sub, rows_chunk, 0)


def _combine(h, y_sorted, pos_tiles, wts_t, mods, gate_k, rows, seq, n_batch):
    tm = TOKEN_TILE
    n_steps = rows // tm
    seg = functools.partial(_seg_of_tile, tm=tm, seq=seq, n_batch=n_batch)
    return pl.pallas_call(
        _combine_kernel,
        grid=(n_steps,),
        in_specs=[
            pl.BlockSpec((1, 1, TOP_K * tm), lambda i: (i, 0, 0)),
            pl.BlockSpec((1, 1, TOP_K * tm), lambda i: (jnp.minimum(i + 1, n_steps - 1), 0, 0)),
            pl.BlockSpec((tm, D_MODEL), lambda i: (i, 0)),
            pl.BlockSpec((tm, TOP_K), lambda i: (i, 0)),
            pl.BlockSpec((1, 1, 1, D_MODEL), lambda i: (seg(i), gate_k, 0, 0)),
            pl.BlockSpec(memory_space=pl.ANY),
        ],
        out_specs=pl.BlockSpec((tm, D_MODEL), lambda i: (i, 0)),
        out_shape=jax.ShapeDtypeStruct((rows, D_MODEL), F32),
        scratch_shapes=[
            pltpu.SMEM((TOP_K * tm,), I32),
            pltpu.VMEM((2, TOP_K, tm * SLAB, LANES), U32),
            pltpu.SemaphoreType.DMA,
            pltpu.SemaphoreType.DMA((2,)),
        ],
        compiler_params=_params("arbitrary", disable_bounds_checks=True),
        name="moe_combine",
    )(pos_tiles, pos_tiles, h, wts_t, mods, y_sorted)


def _dispatch_plan(idx, rank, counts_f, rows):
    tm = EXPERT_TILE
    counts = counts_f[:, 0].astype(I32)
    padded = (counts + tm - 1) // tm * tm
    ends = jnp.cumsum(padded)
    offs = ends - padded
    expert = jnp.arange(N_EXPERTS, dtype=I32)[:, None, None]
    pos = rank + jnp.sum(jnp.where(idx[None] == expert, offs[:, None, None], 0), axis=0)
    n_tok_tiles = rows // TOKEN_TILE
    pos_tiles = pos.reshape(TOP_K, n_tok_tiles, TOKEN_TILE).transpose(1, 0, 2).reshape(n_tok_tiles, 1, TOP_K * TOKEN_TILE)
    n_slots = TOP_K * rows + N_EXPERTS * tm
    n_used = ends[-1] // tm
    tile_start = jnp.arange(n_slots // tm, dtype=I32) * tm
    tile_expert = jnp.sum((tile_start[:, None] >= ends[None, :]).astype(I32), axis=1)
    last_expert = jnp.sum((((n_used - 1) * tm) >= ends).astype(I32))
    tile_expert = jnp.where(tile_start < ends[-1], tile_expert, last_expert).astype(I32)
    return pos_tiles, counts, offs.astype(I32), tile_expert, n_used.astype(I32).reshape(1), n_slots


def _moe(h, g2, mods, layer, rows, seq, n_batch, w_router, e_bias, wg, wu, wd, sg, su, sd):
    xn, xw, idx, wts, rank, counts_f = _norm_mod(h, g2, mods, 3, 4, rows, seq, n_batch, router=(w_router, e_bias))
    hid_s = _swiglu(xn, sg, su, layer, rows)
    h = _mm_residual(hid_s, sd, layer, h, mods, 5, rows, seq, n_batch)
    pos_tiles, counts, offs, tile_expert, n_used, n_slots = _dispatch_plan(idx, rank, counts_f, rows)
    x_sorted = _dispatch(xw, pos_tiles, counts, offs, n_slots)
    y_sorted = _experts(x_sorted, tile_expert, n_used, wg, wu, wd, layer)
    return _combine(h, y_sorted, pos_tiles, wts.T, mods, 5, rows, seq, n_batch)


def _rope_tables(n_batch, seq, total_rows):
    pos = jnp.arange(seq)
    row = (pos // GRID_W).astype(F32)
    col = (pos % GRID_W).astype(F32)
    inv = ROPE_THETA ** (-jnp.arange(0, ROPE_AXIS_DIM, 2, dtype=F32) / ROPE_AXIS_DIM)
    ar = row[:, None] * inv
    ac = col[:, None] * inv
    cos = jnp.concatenate([jnp.cos(ar), jnp.cos(ar), jnp.cos(ac), jnp.cos(ac)], axis=-1)
    sin = jnp.concatenate([-jnp.sin(ar), jnp.sin(ar), -jnp.sin(ac), jnp.sin(ac)], axis=-1)
    n_ctx_rows = total_rows - n_batch * seq
    cos = jnp.concatenate([jnp.tile(cos, (n_batch, 1)), jnp.ones((n_ctx_rows, HEAD_DIM), F32)], axis=0)
    sin = jnp.concatenate([jnp.tile(sin, (n_batch, 1)), jnp.zeros((n_ctx_rows, HEAD_DIM), F32)], axis=0)
    return cos, sin


def kernel(x, c, ctx, c_ctx, ada_down, ada_up, ada_bias, norm1_g, norm2_g, swa_w_qkv, swa_w_o, swa_q_gain, swa_k_gain, swa_sink, fnet_w_o, nat_w_qkv, nat_w_o, nat_q_gain, nat_k_gain, nat_rpb, moe_router, moe_bias, moe_w_gate, moe_w_up, moe_w_down, shared_w_gate, shared_w_up, shared_w_down):
    n_batch, seq, _ = x.shape
    ctx_len = ctx.shape[1]
    lat_rows = n_batch * seq
    all_rows = lat_rows + n_batch * ctx_len
    assert n_batch + 1 <= COND_ROWS and seq % ROW_TILE == 0 and lat_rows % ROW_TILE == 0 and all_rows % ROW_TILE == 0
    assert seq // GRID_W >= NAT_KH and seq % NAT_BLOCK == 0 and ctx_len == NAT_BLOCK

    h = jnp.concatenate([x.reshape(lat_rows, D_MODEL), ctx.reshape(n_batch * ctx_len, D_MODEL)], axis=0)
    cond = jnp.concatenate([c, c_ctx[None, :], jnp.zeros((COND_ROWS - n_batch - 1, D_MODEL), F32)], axis=0)
    mods_all = _ada_mods(cond, ada_down, ada_up, ada_bias).reshape(DEPTH, COND_ROWS, N_MOD, 1, D_MODEL)
    cos, sin = _rope_tables(n_batch, seq, all_rows)
    ones_row = jnp.ones((N_HEADS * HEAD_DIM,), F32)

    slot = [0, 0, 0]
    for layer in range(DEPTH):
        kind = layer % 3
        s = slot[kind]
        slot[kind] += 1
        ctx_out = layer < DEPTH - 1
        rows = all_rows if ctx_out else lat_rows
        mods = mods_all[layer]
        common = dict(n_batch=n_batch, seq=seq, ctx_len=ctx_len, out_rows=rows)
        xn = _norm_mod(h, norm1_g[layer], mods, 0, 1, all_rows, seq, n_batch)
        if kind == 0:
            nq, nkv = N_HEADS * HEAD_DIM, N_KV_HEADS * HEAD_DIM
            gain = jnp.concatenate([jnp.tile(swa_q_gain[s], N_HEADS), jnp.tile(swa_k_gain[s], N_KV_HEADS),
                                    ones_row[:nkv]])[None, :]
            qkv = _qkv_proj(xn, swa_w_qkv, s, gain, cos, sin, nq + nkv, rope=True)
            att = dict(kind="swa", k_col0=N_HEADS, v_col0=N_HEADS + N_KV_HEADS, group=GQA_GROUP,
                       sink=swa_sink[s], **common)
            o = _attention(qkv, latent=True, **att)
            if ctx_out:
                o = _attention(qkv, latent=False, prev=o, **att)
            w_o = swa_w_o
        elif kind == 1:
            o = _fourier_mix(xn, all_rows, n_batch, seq, ctx_len, ctx_out)
            w_o = fnet_w_o
        else:
            width = N_HEADS * HEAD_DIM
            gain = jnp.concatenate([jnp.tile(nat_q_gain[s], N_HEADS), jnp.tile(nat_k_gain[s], N_HEADS),
                                    ones_row])[None, :]
            qkv = _qkv_proj(xn, nat_w_qkv, s, gain, cos, sin, 2 * width, rope=False)
            att = dict(kind="nat", k_col0=N_HEADS, v_col0=2 * N_HEADS, group=1, **common)
            o = _attention(qkv, latent=True, bias=_nat_bias_table(nat_rpb[s]), **att)
            if ctx_out:
                o = _attention(qkv, latent=False, prev=o, **att)
            w_o = nat_w_o
        h = _mm_residual(o, w_o, s, h, mods, 2, rows, seq, n_batch)
        h = _moe(h, norm2_g[layer], mods, layer, rows, seq, n_batch, moe_router[layer], moe_bias[layer],
                 moe_w_gate, moe_w_up, moe_w_down, shared_w_gate, shared_w_up, shared_w_down)
    return h[:lat_rows].reshape(n_batch, seq, D_MODEL)
```

```python
import functools

import numpy as np
import jax
import jax.numpy as jnp
from jax import lax
from jax.experimental import pallas as pl
from jax.experimental.pallas import tpu as pltpu

F32 = jnp.float32
BF16 = jnp.bfloat16
U32 = jnp.uint32
I32 = jnp.int32

D_MODEL = 4096
DEPTH = 4
GRID_W = 64
HEAD_DIM = 128
N_HEADS = 32
N_KV_HEADS = 8
GQA_GROUP = N_HEADS // N_KV_HEADS
WINDOW = 128
ROPE_THETA = 10000.0
ROPE_AXIS_DIM = HEAD_DIM // 2
NAT_KH = 8
NAT_KW = 16
NAT_ROWS_PER_BLOCK = 4
NAT_BLOCK = NAT_ROWS_PER_BLOCK * GRID_W
FOURIER_GROUPS = 8
FOURIER_GROUP_DIM = D_MODEL // FOURIER_GROUPS
N_EXPERTS = 32
TOP_K = 4
N_EXPERT_GROUPS = 4
GROUP_SIZE = N_EXPERTS // N_EXPERT_GROUPS
TOPK_GROUPS = 2
EXPERT_HIDDEN = 256
SHARED_HIDDEN = 1024
ROUTED_SCALE = 2.5
ADA_RANK = 256
N_MOD = 6
EPS = 1e-6
MASKED = -1e30

VMEM_LIMIT_BYTES = 56 * 1024 * 1024
ROW_TILE = 1024
COL_TILE = 512
EXPERT_TILE = 256
TOKEN_TILE = 256
HALF_D = D_MODEL // 2
LANES = 128
SLAB = HALF_D // LANES
COND_ROWS = 16
DMA_UNROLL = 8
N_DISPATCH_BUFS = 3
ATTN_HEADS_PER_STEP = 4


def _params(*sem, **kw):
    return pltpu.CompilerParams(dimension_semantics=sem, vmem_limit_bytes=VMEM_LIMIT_BYTES, **kw)


def _dot(a, b):
    return jnp.dot(a, b, preferred_element_type=F32)


def _dot_nt(a, b):
    return lax.dot_general(a, b, (((1,), (1,)), ((), ())), preferred_element_type=F32)


def _silu(x):
    return x * (1.0 / (1.0 + jnp.exp(-x)))


def _pack_halves(lo, hi):
    lo_bits = lax.bitcast_convert_type(lo.astype(BF16).astype(F32), U32)
    hi_bits = lax.bitcast_convert_type(hi.astype(BF16).astype(F32), U32)
    return (lo_bits >> 16) | hi_bits


def _unpack_halves(word):
    lo = lax.bitcast_convert_type(word << 16, F32)
    hi = lax.bitcast_convert_type(word & jnp.uint32(0xFFFF0000), F32)
    return lo, hi


def _slab_rows(first_token, n_tokens, part):
    return pl.ds(first_token * SLAB + part, n_tokens, stride=SLAB)


def _store_slabs(ref, packed):
    n = packed.shape[0]
    for part in range(SLAB):
        ref[_slab_rows(0, n, part), :] = packed[:, part * LANES:(part + 1) * LANES]


def _load_slabs(ref, n):
    return jnp.concatenate([ref[_slab_rows(0, n, part), :] for part in range(SLAB)], axis=1)


def _ada_kernel(cond_ref, down_ref, up_ref, bias_ref, o_ref):
    c = cond_ref[...]
    z = _dot(_silu(c).astype(BF16), down_ref[0].astype(BF16))
    o_ref[0] = _dot(z.astype(BF16), up_ref[0].astype(BF16)) + bias_ref[0]


def _ada_mods(cond, ada_down, ada_up, ada_bias):
    return pl.pallas_call(
        _ada_kernel,
        grid=(DEPTH, N_MOD),
        in_specs=[
            pl.BlockSpec((COND_ROWS, D_MODEL), lambda l, j: (0, 0)),
            pl.BlockSpec((1, D_MODEL, ADA_RANK), lambda l, j: (l, 0, 0)),
            pl.BlockSpec((1, ADA_RANK, D_MODEL), lambda l, j: (l, 0, j)),
            pl.BlockSpec((1, 1, D_MODEL), lambda l, j: (l, 0, j)),
        ],
        out_specs=pl.BlockSpec((1, COND_ROWS, D_MODEL), lambda l, j: (l, 0, j)),
        out_shape=jax.ShapeDtypeStruct((DEPTH, COND_ROWS, N_MOD * D_MODEL), F32),
        compiler_params=_params("arbitrary", "arbitrary"),
        name="ada_mods",
    )(cond, ada_down, ada_up, ada_bias[:, None, :])


def _norm_mod_tile(h_ref, g_ref, shift_ref, scale_ref):
    x = h_ref[...]
    ms = jnp.mean(x * x, axis=-1, keepdims=True)
    y = x * lax.rsqrt(ms + EPS) * g_ref[...]
    return y * (1.0 + scale_ref[0, 0]) + shift_ref[0, 0]


def _norm_mod_kernel(h_ref, g_ref, shift_ref, scale_ref, o_ref):
    o_ref[...] = _norm_mod_tile(h_ref, g_ref, shift_ref, scale_ref).astype(BF16)


def _split_bf16(x):
    hi = x.astype(BF16)
    lo = (x - hi.astype(F32)).astype(BF16)
    return hi, lo


def _first_index_of_max(v, idx, n):
    m = jnp.max(v, axis=0, keepdims=True)
    i = jnp.min(jnp.where(v == m, idx, n), axis=0, keepdims=True)
    return m, i


def _norm_route_kernel(h_ref, g_ref, shift_ref, scale_ref, wr_ref, eb_ref,
                       o_ref, xw_ref, idx_ref, wts_ref, rank_ref, cnt_ref, run_ref):
    @pl.when(pl.program_id(0) == 0)
    def _():
        run_ref[...] = jnp.zeros_like(run_ref)

    xn = _norm_mod_tile(h_ref, g_ref, shift_ref, scale_ref)
    o_ref[...] = xn.astype(BF16)
    _store_slabs(xw_ref, _pack_halves(xn[:, :HALF_D], xn[:, HALF_D:]))
    x_hi, x_lo = _split_bf16(xn)
    w_hi, w_lo = _split_bf16(wr_ref[...])
    logits = _dot(x_hi, w_hi) + _dot(x_lo, w_hi) + _dot(x_hi, w_lo)
    lt = logits.T[:N_EXPERTS]
    scores = 1.0 / (1.0 + jnp.exp(-lt))
    sel = scores + eb_ref[...]
    n_tok = sel.shape[1]
    sub = lax.broadcasted_iota(I32, (GROUP_SIZE, n_tok), 0).astype(F32)
    grp_scores = []
    for g in range(N_EXPERT_GROUPS):
        sg = sel[g * GROUP_SIZE:(g + 1) * GROUP_SIZE]
        m1, i1 = _first_index_of_max(sg, sub, float(GROUP_SIZE))
        m2 = jnp.max(jnp.where(sub == i1, -jnp.inf, sg), axis=0, keepdims=True)
        grp_scores.append(m1 + m2)
    eidx_i = lax.broadcasted_iota(I32, sel.shape, 0)
    eidx = eidx_i.astype(F32)
    egrp = (eidx_i // GROUP_SIZE).astype(F32)
    best = jnp.full((1, n_tok), -jnp.inf, F32)
    for g in range(N_EXPERT_GROUPS):
        best = jnp.maximum(best, grp_scores[g])
    g1 = jnp.full((1, n_tok), float(N_EXPERT_GROUPS), F32)
    for g in reversed(range(N_EXPERT_GROUPS)):
        g1 = jnp.where(grp_scores[g] == best, float(g), g1)
    second = jnp.full((1, n_tok), -jnp.inf, F32)
    for g in range(N_EXPERT_GROUPS):
        second = jnp.maximum(second, jnp.where(g1 == float(g), -jnp.inf, grp_scores[g]))
    g2 = jnp.full((1, n_tok), float(N_EXPERT_GROUPS), F32)
    for g in reversed(range(N_EXPERT_GROUPS)):
        g2 = jnp.where((grp_scores[g] == second) & (g1 != float(g)), float(g), g2)
    masked = jnp.where((egrp == g1) | (egrp == g2), sel, -jnp.inf)
    picks, weights, hits = [], [], []
    for _ in range(TOP_K):
        _, ik = _first_index_of_max(masked, eidx, float(N_EXPERTS))
        hit = eidx == ik
        weights.append(jnp.sum(jnp.where(hit, scores, 0.0), axis=0, keepdims=True))
        picks.append(ik)
        hits.append(hit)
        masked = jnp.where(hit, -jnp.inf, masked)
    w = jnp.concatenate(weights, axis=0)
    idx_ref[...] = jnp.concatenate(picks, axis=0).astype(I32)
    wts_ref[...] = w / jnp.sum(w, axis=0, keepdims=True) * ROUTED_SCALE
    member = jnp.zeros(sel.shape, F32)
    for hit in hits:
        member = jnp.where(hit, 1.0, member)
    earlier = lax.broadcasted_iota(I32, (n_tok, n_tok), 0) < lax.broadcasted_iota(I32, (n_tok, n_tok), 1)
    before = run_ref[...] + _dot(member.astype(BF16), jnp.where(earlier, 1.0, 0.0).astype(BF16))
    rank_ref[...] = jnp.concatenate(
        [jnp.sum(jnp.where(hit, before, 0.0), axis=0, keepdims=True) for hit in hits], axis=0).astype(I32)
    run_ref[...] = run_ref[...] + jnp.sum(member, axis=1, keepdims=True)
    cnt_ref[...] = run_ref[...]


def _seg_of_tile(i, tm, seq, n_batch):
    return jnp.minimum((i * tm) // seq, n_batch)


def _norm_mod(h, g, mods, shift_k, scale_k, rows, seq, n_batch, router=None):
    tm = TOKEN_TILE
    seg = functools.partial(_seg_of_tile, tm=tm, seq=seq, n_batch=n_batch)
    in_specs = [
        pl.BlockSpec((tm, D_MODEL), lambda i: (i, 0)),
        pl.BlockSpec((1, D_MODEL), lambda i: (0, 0)),
        pl.BlockSpec((1, 1, 1, D_MODEL), lambda i: (seg(i), shift_k, 0, 0)),
        pl.BlockSpec((1, 1, 1, D_MODEL), lambda i: (seg(i), scale_k, 0, 0)),
    ]
    xn_spec = pl.BlockSpec((tm, D_MODEL), lambda i: (i, 0))
    xn_shape = jax.ShapeDtypeStruct((rows, D_MODEL), BF16)
    if router is None:
        return pl.pallas_call(
            _norm_mod_kernel, grid=(rows // tm,), in_specs=in_specs, out_specs=xn_spec, out_shape=xn_shape,
            compiler_params=_params("arbitrary"), name="norm_mod",
        )(h, g.reshape(1, D_MODEL), mods, mods)
    w_router, e_bias = router
    w_pad = jnp.pad(w_router, ((0, 0), (0, 128 - N_EXPERTS)))
    in_specs += [
        pl.BlockSpec((D_MODEL, 128), lambda i: (0, 0)),
        pl.BlockSpec((N_EXPERTS, 1), lambda i: (0, 0)),
    ]
    per_tok = pl.BlockSpec((TOP_K, tm), lambda i: (0, i))
    return pl.pallas_call(
        _norm_route_kernel, grid=(rows // tm,), in_specs=in_specs,
        out_specs=[xn_spec, pl.BlockSpec((tm * SLAB, LANES), lambda i: (i, 0)), per_tok, per_tok, per_tok,
                   pl.BlockSpec((N_EXPERTS, 1), lambda i: (0, 0))],
        out_shape=[xn_shape, jax.ShapeDtypeStruct((rows * SLAB, LANES), U32),
                   jax.ShapeDtypeStruct((TOP_K, rows), I32), jax.ShapeDtypeStruct((TOP_K, rows), F32),
                   jax.ShapeDtypeStruct((TOP_K, rows), I32), jax.ShapeDtypeStruct((N_EXPERTS, 1), F32)],
        scratch_shapes=[pltpu.VMEM((N_EXPERTS, 1), F32)],
        compiler_params=_params("arbitrary"), name="norm_route",
    )(h, g.reshape(1, D_MODEL), mods, mods, w_pad, e_bias.reshape(N_EXPERTS, 1))


def _cast_weight_once(w_ref, w_scr):
    @pl.when(pl.program_id(1) == 0)
    def _():
        w_scr[...] = w_ref[0].astype(BF16)


def _qkv_kernel(a_ref, w_ref, gain_ref, cos_ref, sin_ref, o_ref, w_scr, *, n_qk_tiles, rope):
    _cast_weight_once(w_ref, w_scr)
    j = pl.program_id(0)
    acc = _dot(a_ref[...], w_scr[...])
    n_heads = acc.shape[1] // HEAD_DIM

    @pl.when(j < n_qk_tiles)
    def _():
        lane = lax.broadcasted_iota(I32, (acc.shape[0], HEAD_DIM), 1)
        quarter = ROPE_AXIS_DIM // 2
        first_half = (lane % ROPE_AXIS_DIM) < quarter
        outs = []
        for hh in range(n_heads):
            x = acc[:, hh * HEAD_DIM:(hh + 1) * HEAD_DIM]
            ms = jnp.mean(x * x, axis=-1, keepdims=True)
            y = x * lax.rsqrt(ms + EPS) * gain_ref[:, hh * HEAD_DIM:(hh + 1) * HEAD_DIM]
            if rope:
                partner = jnp.where(first_half, pltpu.roll(y, HEAD_DIM - quarter, 1), pltpu.roll(y, quarter, 1))
                y = y * cos_ref[...] + partner * sin_ref[...]
            outs.append(y.astype(BF16))
        o_ref[...] = jnp.concatenate(outs, axis=1)

    @pl.when(j >= n_qk_tiles)
    def _():
        o_ref[...] = acc.astype(BF16)


def _qkv_proj(xn, w_stack, w_slot, gain_row, cos, sin, n_qk_cols, rope):
    rows, width = xn.shape[0], w_stack.shape[2]
    tm, tn = ROW_TILE, COL_TILE
    return pl.pallas_call(
        functools.partial(_qkv_kernel, n_qk_tiles=n_qk_cols // tn, rope=rope),
        grid=(width // tn, rows // tm),
        in_specs=[
            pl.BlockSpec((tm, D_MODEL), lambda j, i: (i, 0)),
            pl.BlockSpec((1, D_MODEL, tn), lambda j, i: (w_slot, 0, j)),
            pl.BlockSpec((1, tn), lambda j, i: (0, j)),
            pl.BlockSpec((tm, HEAD_DIM), lambda j, i: (i, 0)),
            pl.BlockSpec((tm, HEAD_DIM), lambda j, i: (i, 0)),
        ],
        out_specs=pl.BlockSpec((tm, tn), lambda j, i: (i, j)),
        out_shape=jax.ShapeDtypeStruct((rows, width), BF16),
        scratch_shapes=[pltpu.VMEM((D_MODEL, tn), BF16)],
        compiler_params=_params("arbitrary", "arbitrary"),
        name="qkv_proj",
    )(xn, w_stack, gain_row, cos, sin)


def _mm_res_kernel(a_ref, w_ref, h_ref, gate_ref, o_ref, w_scr):
    _cast_weight_once(w_ref, w_scr)
    o_ref[...] = h_ref[...] + gate_ref[0, 0] * _dot(a_ref[...], w_scr[...])


def _mm_residual(a, w_stack, w_slot, h, mods, gate_k, rows, seq, n_batch):
    _, k_dim, width = w_stack.shape
    tm, tn = ROW_TILE, COL_TILE
    seg = functools.partial(_seg_of_tile, tm=tm, seq=seq, n_batch=n_batch)
    return pl.pallas_call(
        _mm_res_kernel,
        grid=(width // tn, rows // tm),
        in_specs=[
            pl.BlockSpec((tm, k_dim), lambda j, i: (i, 0)),
            pl.BlockSpec((1, k_dim, tn), lambda j, i: (w_slot, 0, j)),
            pl.BlockSpec((tm, tn), lambda j, i: (i, j)),
            pl.BlockSpec((1, 1, 1, tn), lambda j, i: (seg(i), gate_k, 0, j)),
        ],
        out_specs=pl.BlockSpec((tm, tn), lambda j, i: (i, j)),
        out_shape=jax.ShapeDtypeStruct((rows, width), F32),
        scratch_shapes=[pltpu.VMEM((k_dim, tn), BF16)],
        compiler_params=_params("arbitrary", "arbitrary"),
        name="mm_residual",
    )(a, w_stack, h, mods)


def _swiglu_kernel(a_ref, wg_ref, wu_ref, o_ref, wg_scr, wu_scr):
    _cast_weight_once(wg_ref, wg_scr)
    _cast_weight_once(wu_ref, wu_scr)
    a = a_ref[...]
    o_ref[...] = (_silu(_dot(a, wg_scr[...])) * _dot(a, wu_scr[...])).astype(BF16)


def _swiglu(a, wg_stack, wu_stack, layer, rows):
    width = wg_stack.shape[2]
    tm, tn = ROW_TILE, COL_TILE // 2
    w_spec = pl.BlockSpec((1, D_MODEL, tn), lambda j, i: (layer, 0, j))
    return pl.pallas_call(
        _swiglu_kernel,
        grid=(width // tn, rows // tm),
        in_specs=[pl.BlockSpec((tm, D_MODEL), lambda j, i: (i, 0)), w_spec, w_spec],
        out_specs=pl.BlockSpec((tm, tn), lambda j, i: (i, j)),
        out_shape=jax.ShapeDtypeStruct((rows, width), BF16),
        scratch_shapes=[pltpu.VMEM((D_MODEL, tn), BF16), pltpu.VMEM((D_MODEL, tn), BF16)],
        compiler_params=_params("arbitrary", "arbitrary"),
        name="shared_swiglu",
    )(a, wg_stack, wu_stack)


def _attn_kernel(*refs, n_local, group, block, kind, has_sink, has_prev, n_blocks, heads):
    refs = list(refs)
    sink_ref = refs.pop(0) if has_sink else None
    q_ref = refs.pop(0)
    k_loc = [refs.pop(0) for _ in range(n_local)]
    v_loc = [refs.pop(0) for _ in range(n_local)]
    kc_ref, vc_ref = refs.pop(0), refs.pop(0)
    bias_ref = refs.pop(0) if kind == "nat" else None
    if has_prev:
        refs.pop(0)
    o_ref = refs.pop(0)
    head0 = pl.program_id(1) * heads
    i = pl.program_id(2)
    scale = HEAD_DIM ** -0.5
    n_rows = group * block

    valid = None
    if n_local:
        row = lax.broadcasted_iota(I32, (n_rows, 1), 0)
        col = lax.broadcasted_iota(I32, (1, n_local * block), 1)
        if kind == "swa":
            rel = col - block - (row % block)
            lo = jnp.where(i == 0, block, 0)
            hi = jnp.where(i == n_blocks - 1, 2 * block, 3 * block)
            valid = (rel >= -WINDOW) & (rel <= WINDOW) & (col >= lo) & (col < hi)
        else:
            q_row = NAT_ROWS_PER_BLOCK * i + row // GRID_W
            k_row = NAT_ROWS_PER_BLOCK * (i - 1) + col // GRID_W
            r0 = jnp.clip(q_row - NAT_KH // 2, 0, n_blocks * NAT_ROWS_PER_BLOCK - NAT_KH)
            valid = (k_row >= r0) & (k_row < r0 + NAT_KH)
    g_of_row = lax.broadcasted_iota(I32, (n_rows, 1), 0) // block if has_sink else None

    outs = []
    for hd in range(heads):
        kv_cols = slice(hd * HEAD_DIM, (hd + 1) * HEAD_DIM)
        if group > 1:
            q = jnp.concatenate([q_ref[:, (hd * group + g) * HEAD_DIM:(hd * group + g + 1) * HEAD_DIM]
                                 for g in range(group)], axis=0)
        else:
            q = q_ref[:, kv_cols]
        s_ctx = _dot_nt(q, kc_ref[:, kv_cols]) * scale
        m = jnp.max(s_ctx, axis=-1, keepdims=True)
        if n_local:
            k = jnp.concatenate([r[:, kv_cols] for r in k_loc], axis=0)
            v = jnp.concatenate([r[:, kv_cols] for r in v_loc], axis=0)
            s_loc = _dot_nt(q, k) * scale
            if kind == "nat":
                s_loc = s_loc + bias_ref[hd]
            s_loc = jnp.where(valid, s_loc, MASKED)
            m = jnp.maximum(m, jnp.max(s_loc, axis=-1, keepdims=True))
        if has_sink:
            sink = jnp.zeros((n_rows, 1), F32)
            for g in range(group):
                sink = jnp.where(g_of_row == g, sink_ref[(head0 + hd) * group + g], sink)
            m = jnp.maximum(m, sink)
        p_ctx = jnp.exp(s_ctx - m)
        denom = jnp.sum(p_ctx, axis=-1, keepdims=True)
        o = _dot(p_ctx.astype(BF16), vc_ref[:, kv_cols])
        if n_local:
            p_loc = jnp.exp(s_loc - m)
            denom = denom + jnp.sum(p_loc, axis=-1, keepdims=True)
            o = o + _dot(p_loc.astype(BF16), v)
        if has_sink:
            denom = denom + jnp.exp(sink - m)
        o = (o * (1.0 / denom)).astype(BF16)
        outs += [o[g * block:(g + 1) * block] for g in range(group)]
    o_ref[...] = jnp.concatenate(outs, axis=1) if len(outs) > 1 else outs[0]


def _attention(qkv, *, kind, latent, n_batch, seq, ctx_len, k_col0, v_col0, group, out_rows,
               sink=None, bias=None, prev=None):
    heads = ATTN_HEADS_PER_STEP
    n_kv = v_col0 - k_col0
    assert n_kv % heads == 0 and k_col0 % heads == 0 and v_col0 % heads == 0
    n_steps_h = n_kv // heads
    kc0, vc0 = k_col0 // heads, v_col0 // heads
    block = (WINDOW if kind == "swa" else NAT_BLOCK) if latent else ctx_len
    n_blocks = seq // block if latent else 1
    n_local = 3 if latent else 0
    ctx_blk0 = n_batch * seq // ctx_len
    q_blk0 = 0 if latent else n_batch * seq // block
    has_sink = sink is not None
    nat_order = kind == "nat"

    def ids(a, b, c):
        return (b, a, c) if nat_order else (a, b, c)

    def q_map(a, b, c):
        bb, hh, ii = ids(a, b, c)
        return (q_blk0 + bb * n_blocks + ii, hh)

    def loc_map(delta, col0):
        def f(a, b, c):
            bb, hh, ii = ids(a, b, c)
            return (bb * n_blocks + jnp.clip(ii + delta, 0, n_blocks - 1), col0 + hh)
        return f

    def ctx_map(col0):
        def f(a, b, c):
            bb, hh, _ = ids(a, b, c)
            return (ctx_blk0 + bb, col0 + hh)
        return f

    in_specs, args = [], []
    if has_sink:
        in_specs.append(pl.BlockSpec(memory_space=pltpu.SMEM))
        args.append(sink.astype(F32))
    in_specs.append(pl.BlockSpec((block, heads * group * HEAD_DIM), q_map))
    args.append(qkv)
    for col0 in ((kc0, vc0) if latent else ()):
        for delta in (-1, 0, 1):
            in_specs.append(pl.BlockSpec((block, heads * HEAD_DIM), loc_map(delta, col0)))
            args.append(qkv)
    for col0 in (kc0, vc0):
        in_specs.append(pl.BlockSpec((ctx_len, heads * HEAD_DIM), ctx_map(col0)))
        args.append(qkv)
    use_bias = nat_order and latent
    if use_bias:
        in_specs.append(pl.BlockSpec((heads, block, 3 * block), lambda a, b, c: (a, 0, 0)))
        args.append(bias)
    aliases = {}
    if prev is not None:
        aliases = {len(args): 0}
        in_specs.append(pl.BlockSpec(memory_space=pl.ANY))
        args.append(prev)
    grid = (n_steps_h, n_batch, n_blocks) if nat_order else (n_batch, n_steps_h, n_blocks)
    kern = functools.partial(_attn_kernel, n_local=n_local, group=group, block=block,
                             kind=("nat" if use_bias else kind if latent else "ctx"),
                             has_sink=has_sink, has_prev=prev is not None, n_blocks=n_blocks, heads=heads)
    return pl.pallas_call(
        kern, grid=grid, in_specs=in_specs,
        out_specs=pl.BlockSpec((block, heads * group * HEAD_DIM), q_map),
        out_shape=jax.ShapeDtypeStruct((out_rows, n_kv * group * HEAD_DIM), BF16),
        input_output_aliases=aliases,
        compiler_params=_params("arbitrary", "arbitrary", "arbitrary"),
        name=f"attn_{kind}_{'latent' if latent else 'ctx'}",
    )(*args)


def _nat_bias_table(rpb):
    q_c = np.arange(GRID_W)[:, None]
    k_c = np.arange(GRID_W)[None, :]
    d_col = np.clip(k_c - q_c + NAT_KW - 1, 0, 2 * NAT_KW - 2)
    c0 = np.clip(q_c - NAT_KW // 2, 0, GRID_W - NAT_KW)
    col_valid = (k_c >= c0) & (k_c < c0 + NAT_KW)
    onehot = (np.arange(2 * NAT_KW - 1)[:, None, None] == d_col[None]).astype(np.float32)
    by_col = jnp.einsum("hrd,dqk->hrqk", rpb.astype(F32), jnp.asarray(onehot), precision=lax.Precision.HIGHEST)
    by_col = jnp.where(col_valid[None, None], by_col, MASKED)
    n_key_rows = 3 * NAT_ROWS_PER_BLOCK
    rows = []
    for q_r in range(NAT_ROWS_PER_BLOCK):
        d_row = [k_r - NAT_ROWS_PER_BLOCK - q_r + NAT_KH - 1 for k_r in range(n_key_rows)]
        rows.append(jnp.concatenate([by_col[:, d] for d in d_row], axis=-1))
    return jnp.concatenate(rows, axis=1)


def _dft_chan_kernel(a_ref, w_ref, o_ref):
    o_ref[0] = _dot(a_ref[...], w_ref[0]).astype(BF16)


def _dft_pos_kernel(cn_ref, sn_ref, xc_ref, xs_ref, *rest, scale):
    o_ref = rest[-1]
    o_ref[...] = ((_dot(cn_ref[...], xc_ref[0]) - _dot(sn_ref[...], xs_ref[0])) * scale).astype(BF16)


def _dft_tables(n):
    k = jnp.arange(n, dtype=I32)
    ang = ((k[:, None] * k[None, :]) % n).astype(F32) * (2.0 * np.pi / n)
    return jnp.cos(ang).astype(BF16), jnp.sin(ang).astype(BF16)


def _fourier_mix(xn, rows, n_batch, seq, ctx_len, ctx_out):
    gd = FOURIER_GROUP_DIM
    cd, sd = _dft_tables(gd)
    chan = jnp.stack([cd, sd])
    tm = ROW_TILE
    xcs = pl.pallas_call(
        _dft_chan_kernel,
        grid=(rows // tm, FOURIER_GROUPS, 2),
        in_specs=[pl.BlockSpec((tm, gd), lambda i, g, t: (i, g)),
                  pl.BlockSpec((1, gd, gd), lambda i, g, t: (t, 0, 0))],
        out_specs=pl.BlockSpec((1, tm, gd), lambda i, g, t: (t, i, g)),
        out_shape=jax.ShapeDtypeStruct((2, rows, D_MODEL), BF16),
        compiler_params=_params("arbitrary", "arbitrary", "arbitrary"),
        name="dft_channels",
    )(xn, chan)

    def pos_call(n, tmo, row_blk0, in_blk0, out_rows, prev):
        cn, sn = _dft_tables(n)
        tn = COL_TILE
        per = n // tmo
        in_specs = [
            pl.BlockSpec((tmo, n), lambda b, i, j: (i, 0)),
            pl.BlockSpec((tmo, n), lambda b, i, j: (i, 0)),
            pl.BlockSpec((1, n, tn), lambda b, i, j: (0, in_blk0 + b, j)),
            pl.BlockSpec((1, n, tn), lambda b, i, j: (1, in_blk0 + b, j)),
        ]
        args = [cn, sn, xcs, xcs]
        aliases = {}
        if prev is not None:
            in_specs.append(pl.BlockSpec(memory_space=pl.ANY))
            args.append(prev)
            aliases = {4: 0}
        return pl.pallas_call(
            functools.partial(_dft_pos_kernel, scale=float((n * gd) ** -0.5)),
            grid=(n_batch, per, D_MODEL // tn),
            in_specs=in_specs,
            out_specs=pl.BlockSpec((tmo, tn), lambda b, i, j: (row_blk0 + b * per + i, j)),
            out_shape=jax.ShapeDtypeStruct((out_rows, D_MODEL), BF16),
            input_output_aliases=aliases,
            compiler_params=_params("arbitrary", "arbitrary", "arbitrary"),
            name=f"dft_positions_{n}",
        )(*args)

    z = pos_call(seq, 512, 0, 0, rows, None)
    if ctx_out:
        z = pos_call(ctx_len, ctx_len, n_batch * seq // ctx_len, n_batch * seq // ctx_len, rows, z)
    return z


def _slab_copy(src_ref, src_token, dst_ref, dst_token, sem):
    src = src_ref.at[pl.ds(pl.multiple_of(src_token * SLAB, SLAB), SLAB)]
    dst = dst_ref.at[pl.ds(pl.multiple_of(dst_token * SLAB, SLAB), SLAB)]
    return pltpu.make_async_copy(src, dst, sem)


def _load_slots(pos_ref, pos_smem, sem):
    cp = pltpu.make_async_copy(pos_ref.at[0, 0], pos_smem, sem)
    cp.start()
    cp.wait()


def _for_token_groups(n_tokens, fn):
    def group(g, c):
        for k in range(TOP_K):
            for u in range(DMA_UNROLL):
                fn(k, g * DMA_UNROLL + u)
        return c
    lax.fori_loop(0, n_tokens // DMA_UNROLL, group, 0)


def _dispatch_kernel(cnt_ref, off_ref, pos_ref, xw_ref, xs_ref,
                     pos_smem, zero_ref, xbuf_ref, sem_idx, sem_load, sem_rows):
    i = pl.program_id(0)
    n_steps = pl.num_programs(0)
    tm = pos_smem.shape[0] // TOP_K

    @pl.when(i == 0)
    def _():
        zero_ref[...] = jnp.zeros_like(zero_ref)

        def per_expert(e, carry):
            count = cnt_ref[e]
            first = off_ref[e] + count
            n_pad = (count + EXPERT_TILE - 1) // EXPERT_TILE * EXPERT_TILE - count

            def start(r, c):
                _slab_copy(zero_ref, 0, xs_ref, first + r, sem_rows.at[0]).start()
                return c

            def wait(r, c):
                _slab_copy(zero_ref, 0, xs_ref, first + r, sem_rows.at[0]).wait()
                return c

            lax.fori_loop(0, n_pad, start, 0)
            lax.fori_loop(0, n_pad, wait, 0)
            return carry

        lax.fori_loop(0, N_EXPERTS, per_expert, 0)

    def load(step):
        which = step % N_DISPATCH_BUFS
        src = xw_ref.at[pl.ds(pl.multiple_of(step * tm * SLAB, tm * SLAB), tm * SLAB)]
        return pltpu.make_async_copy(src, xbuf_ref.at[which], sem_load.at[which])

    def drain(step):
        which = step % N_DISPATCH_BUFS
        _for_token_groups(tm, lambda k, t: _slab_copy(xbuf_ref.at[which], t, xs_ref, 0, sem_rows.at[which]).wait())

    @pl.when(i == 0)
    def _():
        load(i).start()

    @pl.when(i >= N_DISPATCH_BUFS - 1)
    def _():
        drain(i - (N_DISPATCH_BUFS - 1))

    @pl.when(i + 1 < n_steps)
    def _():
        load(i + 1).start()

    load(i).wait()
    _load_slots(pos_ref, pos_smem, sem_idx)
    cur = i % N_DISPATCH_BUFS
    _for_token_groups(tm, lambda k, t: _slab_copy(
        xbuf_ref.at[cur], t, xs_ref, pos_smem[k * tm + t], sem_rows.at[cur]).start())

    @pl.when(i == n_steps - 1)
    def _():
        for back in reversed(range(N_DISPATCH_BUFS - 1)):
            @pl.when(i - back >= 0)
            def _(back=back):
                drain(i - back)


def _dispatch(xw, pos_tiles, counts, offs, n_slots):
    rows = xw.shape[0] // SLAB
    tm = TOKEN_TILE
    grid_spec = pltpu.PrefetchScalarGridSpec(
        num_scalar_prefetch=2,
        grid=(rows // tm,),
        in_specs=[
            pl.BlockSpec((1, 1, TOP_K * tm), lambda i, cnt, off: (i, 0, 0)),
            pl.BlockSpec(memory_space=pl.ANY),
        ],
        out_specs=pl.BlockSpec(memory_space=pl.ANY),
        scratch_shapes=[
            pltpu.SMEM((TOP_K * tm,), I32),
            pltpu.VMEM((SLAB, LANES), U32),
            pltpu.VMEM((N_DISPATCH_BUFS, tm * SLAB, LANES), U32),
            pltpu.SemaphoreType.DMA,
            pltpu.SemaphoreType.DMA((N_DISPATCH_BUFS,)),
            pltpu.SemaphoreType.DMA((N_DISPATCH_BUFS,)),
        ],
    )
    return pl.pallas_call(
        _dispatch_kernel, grid_spec=grid_spec,
        out_shape=jax.ShapeDtypeStruct((n_slots * SLAB, LANES), U32),
        compiler_params=_params("arbitrary", disable_bounds_checks=True),
        name="moe_dispatch",
    )(counts, offs, pos_tiles, xw)


def _expert_kernel(te_ref, nu_ref, x_ref, wg_ref, wu_ref, wd_ref, o_ref, wgu_scr, wd_scr):
    j = pl.program_id(0)

    @pl.when(j < nu_ref[0])
    def _():
        @pl.when((j == 0) | (te_ref[j] != te_ref[jnp.maximum(j - 1, 0)]))
        def _():
            wgu_scr[:, :EXPERT_HIDDEN] = wg_ref[0, 0].astype(BF16)
            wgu_scr[:, EXPERT_HIDDEN:] = wu_ref[0, 0].astype(BF16)
            wd_scr[...] = wd_ref[0, 0].astype(BF16)

        x_lo, x_hi = _unpack_halves(_load_slabs(x_ref, EXPERT_TILE))
        x_lo, x_hi = x_lo.astype(BF16), x_hi.astype(BF16)
        gate_up = _dot(x_lo, wgu_scr[:HALF_D]) + _dot(x_hi, wgu_scr[HALF_D:])
        hid = (_silu(gate_up[:, :EXPERT_HIDDEN]) * gate_up[:, EXPERT_HIDDEN:]).astype(BF16)
        _store_slabs(o_ref, _pack_halves(_dot(hid, wd_scr[:, :HALF_D]), _dot(hid, wd_scr[:, HALF_D:])))


def _experts(x_sorted, tile_expert, n_used, wg, wu, wd, layer):
    n_rows = x_sorted.shape[0] // SLAB
    tm = EXPERT_TILE

    def row_map(j, te, nu):
        return (jnp.minimum(j, nu[0] - 1), 0)

    def w_map(j, te, nu):
        return (layer, te[j], 0, 0)

    grid_spec = pltpu.PrefetchScalarGridSpec(
        num_scalar_prefetch=2,
        grid=(n_rows // tm,),
        in_specs=[
            pl.BlockSpec((tm * SLAB, LANES), row_map),
            pl.BlockSpec((1, 1, D_MODEL, EXPERT_HIDDEN), w_map),
            pl.BlockSpec((1, 1, D_MODEL, EXPERT_HIDDEN), w_map),
            pl.BlockSpec((1, 1, EXPERT_HIDDEN, D_MODEL), w_map),
        ],
        out_specs=pl.BlockSpec((tm * SLAB, LANES), row_map),
        scratch_shapes=[
            pltpu.VMEM((D_MODEL, 2 * EXPERT_HIDDEN), BF16),
            pltpu.VMEM((EXPERT_HIDDEN, D_MODEL), BF16),
        ],
    )
    return pl.pallas_call(
        _expert_kernel, grid_spec=grid_spec,
        out_shape=jax.ShapeDtypeStruct((n_rows * SLAB, LANES), U32),
        compiler_params=_params("arbitrary"),
        name="routed_experts",
    )(tile_expert, n_used, x_sorted, wg, wu, wd)


def _combine_kernel(pos_ref, nxt_ref, h_ref, w_ref, gate_ref, y_ref, o_ref, pos_smem, buf_ref, sem_idx, sem_rows):
    i = pl.program_id(0)
    n_steps = pl.num_programs(0)
    tm = h_ref.shape[0]
    cur = i % 2

    def gather(slot_ref, which):
        _load_slots(slot_ref, pos_smem, sem_idx)
        _for_token_groups(tm, lambda k, t: _slab_copy(
            y_ref, pos_smem[k * tm + t], buf_ref.at[which, k], t, sem_rows.at[which]).start())

    @pl.when(i == 0)
    def _():
        gather(pos_ref, 0)

    @pl.when(i + 1 < n_steps)
    def _():
        gather(nxt_ref, 1 - cur)

    _for_token_groups(tm, lambda k, t: _slab_copy(y_ref, 0, buf_ref.at[cur, k], t, sem_rows.at[cur]).wait())
    sub = 8

    def rows_chunk(r, c):
        first = pl.multiple_of(r * sub, sub)
        rs = pl.ds(first, sub)
        w = w_ref[rs, :]
        wk = [jnp.broadcast_to(w[:, k:k + 1], (sub, LANES)) for k in range(TOP_K)]
        for part in range(SLAB):
            acc_lo = jnp.zeros((sub, LANES), F32)
            acc_hi = jnp.zeros((sub, LANES), F32)
            for k in range(TOP_K):
                y_lo, y_hi = _unpack_halves(buf_ref[cur, k, _slab_rows(first, sub, part), :])
                acc_lo = acc_lo + wk[k] * y_lo
                acc_hi = acc_hi + wk[k] * y_hi
            lo = pl.ds(part * LANES, LANES)
            hi = pl.ds(HALF_D + part * LANES, LANES)
            o_ref[rs, lo] = h_ref[rs, lo] + gate_ref[0, 0, :, lo] * acc_lo
            o_ref[rs, hi] = h_ref[rs, hi] + gate_ref[0, 0, :, hi] * acc_hi
        return c

    lax.fori_loop(0, tm // sub, rows_chunk, 0)


def _combine(h, y_sorted, pos_tiles, wts_t, mods, gate_k, rows, seq, n_batch):
    tm = TOKEN_TILE
    n_steps = rows // tm
    seg = functools.partial(_seg_of_tile, tm=tm, seq=seq, n_batch=n_batch)
    return pl.pallas_call(
        _combine_kernel,
        grid=(n_steps,),
        in_specs=[
            pl.BlockSpec((1, 1, TOP_K * tm), lambda i: (i, 0, 0)),
            pl.BlockSpec((1, 1, TOP_K * tm), lambda i: (jnp.minimum(i + 1, n_steps - 1), 0, 0)),
            pl.BlockSpec((tm, D_MODEL), lambda i: (i, 0)),
            pl.BlockSpec((tm, TOP_K), lambda i: (i, 0)),
            pl.BlockSpec((1, 1, 1, D_MODEL), lambda i: (seg(i), gate_k, 0, 0)),
            pl.BlockSpec(memory_space=pl.ANY),
        ],
        out_specs=pl.BlockSpec((tm, D_MODEL), lambda i: (i, 0)),
        out_shape=jax.ShapeDtypeStruct((rows, D_MODEL), F32),
        scratch_shapes=[
            pltpu.SMEM((TOP_K * tm,), I32),
            pltpu.VMEM((2, TOP_K, tm * SLAB, LANES), U32),
            pltpu.SemaphoreType.DMA,
            pltpu.SemaphoreType.DMA((2,)),
        ],
        compiler_params=_params("arbitrary", disable_bounds_checks=True),
        name="moe_combine",
    )(pos_tiles, pos_tiles, h, wts_t, mods, y_sorted)


def _dispatch_plan(idx, rank, counts_f, rows):
    tm = EXPERT_TILE
    counts = counts_f[:, 0].astype(I32)
    padded = (counts + tm - 1) // tm * tm
    ends = jnp.cumsum(padded)
    offs = ends - padded
    expert = jnp.arange(N_EXPERTS, dtype=I32)[:, None, None]
    pos = rank + jnp.sum(jnp.where(idx[None] == expert, offs[:, None, None], 0), axis=0)
    n_tok_tiles = rows // TOKEN_TILE
    pos_tiles = pos.reshape(TOP_K, n_tok_tiles, TOKEN_TILE).transpose(1, 0, 2).reshape(n_tok_tiles, 1, TOP_K * TOKEN_TILE)
    n_slots = TOP_K * rows + N_EXPERTS * tm
    n_used = ends[-1] // tm
    tile_start = jnp.arange(n_slots // tm, dtype=I32) * tm
    tile_expert = jnp.sum((tile_start[:, None] >= ends[None, :]).astype(I32), axis=1)
    last_expert = jnp.sum((((n_used - 1) * tm) >= ends).astype(I32))
    tile_expert = jnp.where(tile_start < ends[-1], tile_expert, last_expert).astype(I32)
    return pos_tiles, counts, offs.astype(I32), tile_expert, n_used.astype(I32).reshape(1), n_slots


def _moe(h, g2, mods, layer, rows, seq, n_batch, w_router, e_bias, wg, wu, wd, sg, su, sd):
    xn, xw, idx, wts, rank, counts_f = _norm_mod(h, g2, mods, 3, 4, rows, seq, n_batch, router=(w_router, e_bias))
    hid_s = _swiglu(xn, sg, su, layer, rows)
    h = _mm_residual(hid_s, sd, layer, h, mods, 5, rows, seq, n_batch)
    pos_tiles, counts, offs, tile_expert, n_used, n_slots = _dispatch_plan(idx, rank, counts_f, rows)
    x_sorted = _dispatch(xw, pos_tiles, counts, offs, n_slots)
    y_sorted = _experts(x_sorted, tile_expert, n_used, wg, wu, wd, layer)
    return _combine(h, y_sorted, pos_tiles, wts.T, mods, 5, rows, seq, n_batch)


def _rope_tables(n_batch, seq, total_rows):
    pos = jnp.arange(seq)
    row = (pos // GRID_W).astype(F32)
    col = (pos % GRID_W).astype(F32)
    inv = ROPE_THETA ** (-jnp.arange(0, ROPE_AXIS_DIM, 2, dtype=F32) / ROPE_AXIS_DIM)
    ar = row[:, None] * inv
    ac = col[:, None] * inv
    cos = jnp.concatenate([jnp.cos(ar), jnp.cos(ar), jnp.cos(ac), jnp.cos(ac)], axis=-1)
    sin = jnp.concatenate([-jnp.sin(ar), jnp.sin(ar), -jnp.sin(ac), jnp.sin(ac)], axis=-1)
    n_ctx_rows = total_rows - n_batch * seq
    cos = jnp.concatenate([jnp.tile(cos, (n_batch, 1)), jnp.ones((n_ctx_rows, HEAD_DIM), F32)], axis=0)
    sin = jnp.concatenate([jnp.tile(sin, (n_batch, 1)), jnp.zeros((n_ctx_rows, HEAD_DIM), F32)], axis=0)
    return cos, sin


def kernel(x, c, ctx, c_ctx, ada_down, ada_up, ada_bias, norm1_g, norm2_g, swa_w_qkv, swa_w_o, swa_q_gain, swa_k_gain, swa_sink, fnet_w_o, nat_w_qkv, nat_w_o, nat_q_gain, nat_k_gain, nat_rpb, moe_router, moe_bias, moe_w_gate, moe_w_up, moe_w_down, shared_w_gate, shared_w_up, shared_w_down):
    n_batch, seq, _ = x.shape
    ctx_len = ctx.shape[1]
    lat_rows = n_batch * seq
    all_rows = lat_rows + n_batch * ctx_len
    assert n_batch + 1 <= COND_ROWS and seq % ROW_TILE == 0 and lat_rows % ROW_TILE == 0 and all_rows % ROW_TILE == 0
    assert seq // GRID_W >= NAT_KH and seq % NAT_BLOCK == 0 and ctx_len == NAT_BLOCK

    h = jnp.concatenate([x.reshape(lat_rows, D_MODEL), ctx.reshape(n_batch * ctx_len, D_MODEL)], axis=0)
    cond = jnp.concatenate([c, c_ctx[None, :], jnp.zeros((COND_ROWS - n_batch - 1, D_MODEL), F32)], axis=0)
    mods_all = _ada_mods(cond, ada_down, ada_up, ada_bias).reshape(DEPTH, COND_ROWS, N_MOD, 1, D_MODEL)
    cos, sin = _rope_tables(n_batch, seq, all_rows)
    ones_row = jnp.ones((N_HEADS * HEAD_DIM,), F32)

    slot = [0, 0, 0]
    for layer in range(DEPTH):
        kind = layer % 3
        s = slot[kind]
        slot[kind] += 1
        ctx_out = layer < DEPTH - 1
        rows = all_rows if ctx_out else lat_rows
        mods = mods_all[layer]
        common = dict(n_batch=n_batch, seq=seq, ctx_len=ctx_len, out_rows=rows)
        xn = _norm_mod(h, norm1_g[layer], mods, 0, 1, all_rows, seq, n_batch)
        if kind == 0:
            nq, nkv = N_HEADS * HEAD_DIM, N_KV_HEADS * HEAD_DIM
            gain = jnp.concatenate([jnp.tile(swa_q_gain[s], N_HEADS), jnp.tile(swa_k_gain[s], N_KV_HEADS),
                                    ones_row[:nkv]])[None, :]
            qkv = _qkv_proj(xn, swa_w_qkv, s, gain, cos, sin, nq + nkv, rope=True)
            att = dict(kind="swa", k_col0=N_HEADS, v_col0=N_HEADS + N_KV_HEADS, group=GQA_GROUP,
                       sink=swa_sink[s], **common)
            o = _attention(qkv, latent=True, **att)
            if ctx_out:
                o = _attention(qkv, latent=False, prev=o, **att)
            w_o = swa_w_o
        elif kind == 1:
            o = _fourier_mix(xn, all_rows, n_batch, seq, ctx_len, ctx_out)
            w_o = fnet_w_o
        else:
            width = N_HEADS * HEAD_DIM
            gain = jnp.concatenate([jnp.tile(nat_q_gain[s], N_HEADS), jnp.tile(nat_k_gain[s], N_HEADS),
                                    ones_row])[None, :]
            qkv = _qkv_proj(xn, nat_w_qkv, s, gain, cos, sin, 2 * width, rope=False)
            att = dict(kind="nat", k_col0=N_HEADS, v_col0=2 * N_HEADS, group=1, **common)
            o = _attention(qkv, latent=True, bias=_nat_bias_table(nat_rpb[s]), **att)
            if ctx_out:
                o = _attention(qkv, latent=False, prev=o, **att)
            w_o = nat_w_o
        h = _mm_residual(o, w_o, s, h, mods, 2, rows, seq, n_batch)
        h = _moe(h, norm2_g[layer], mods, layer, rows, seq, n_batch, moe_router[layer], moe_bias[layer],
                 moe_w_gate, moe_w_up, moe_w_down, shared_w_gate, shared_w_up, shared_w_down)
    return h[:lat_rows].reshape(n_batch, seq, D_MODEL)
```

```python
import functools

import numpy as np
import jax
import jax.numpy as jnp
from jax import lax
from jax.experimental import pallas as pl
from jax.experimental.pallas import tpu as pltpu

F32 = jnp.float32
BF16 = jnp.bfloat16
U32 = jnp.uint32
I32 = jnp.int32

D_MODEL = 4096
DEPTH = 4
GRID_W = 64
HEAD_DIM = 128
N_HEADS = 32
N_KV_HEADS = 8
GQA_GROUP = N_HEADS // N_KV_HEADS
WINDOW = 128
ROPE_THETA = 10000.0
ROPE_AXIS_DIM = HEAD_DIM // 2
NAT_KH = 8
NAT_KW = 16
NAT_ROWS_PER_BLOCK = 4
NAT_BLOCK = NAT_ROWS_PER_BLOCK * GRID_W
FOURIER_GROUPS = 8
FOURIER_GROUP_DIM = D_MODEL // FOURIER_GROUPS
N_EXPERTS = 32
TOP_K = 4
N_EXPERT_GROUPS = 4
GROUP_SIZE = N_EXPERTS // N_EXPERT_GROUPS
TOPK_GROUPS = 2
EXPERT_HIDDEN = 256
SHARED_HIDDEN = 1024
ROUTED_SCALE = 2.5
ADA_RANK = 256
N_MOD = 6
EPS = 1e-6
MASKED = -1e30

VMEM_LIMIT_BYTES = 56 * 1024 * 1024
ROW_TILE = 1024
COL_TILE = 512
EXPERT_TILE = 256
TOKEN_TILE = 256
HALF_D = D_MODEL // 2
LANES = 128
SLAB = HALF_D // LANES
COND_ROWS = 16
DMA_UNROLL = 8
N_DISPATCH_BUFS = 3
ATTN_HEADS_PER_STEP = 4


def _params(*sem, **kw):
    return pltpu.CompilerParams(dimension_semantics=sem, vmem_limit_bytes=VMEM_LIMIT_BYTES, **kw)


def _dot(a, b):
    return jnp.dot(a, b, preferred_element_type=F32)


def _dot_nt(a, b):
    return lax.dot_general(a, b, (((1,), (1,)), ((), ())), preferred_element_type=F32)


def _silu(x):
    return x * (1.0 / (1.0 + jnp.exp(-x)))


def _pack_halves(lo, hi):
    lo_bits = lax.bitcast_convert_type(lo.astype(BF16).astype(F32), U32)
    hi_bits = lax.bitcast_convert_type(hi.astype(BF16).astype(F32), U32)
    return (lo_bits >> 16) | hi_bits


def _unpack_halves(word):
    lo = lax.bitcast_convert_type(word << 16, F32)
    hi = lax.bitcast_convert_type(word & jnp.uint32(0xFFFF0000), F32)
    return lo, hi


def _slab_rows(first_token, n_tokens, part):
    return pl.ds(first_token * SLAB + part, n_tokens, stride=SLAB)


def _store_slabs(ref, packed, first_token=0):
    n = packed.shape[0]
    for part in range(SLAB):
        ref[_slab_rows(first_token, n, part), :] = packed[:, part * LANES:(part + 1) * LANES]


def _load_slabs(ref, n, first_token=0):
    return jnp.concatenate([ref[_slab_rows(first_token, n, part), :] for part in range(SLAB)], axis=1)


def _ada_kernel(cond_ref, down_ref, up_ref, bias_ref, o_ref):
    c = cond_ref[...]
    z = _dot(_silu(c).astype(BF16), down_ref[0].astype(BF16))
    o_ref[0] = _dot(z.astype(BF16), up_ref[0].astype(BF16)) + bias_ref[0]


def _ada_mods(cond, ada_down, ada_up, ada_bias):
    return pl.pallas_call(
        _ada_kernel,
        grid=(DEPTH, N_MOD),
        in_specs=[
            pl.BlockSpec((COND_ROWS, D_MODEL), lambda l, j: (0, 0)),
            pl.BlockSpec((1, D_MODEL, ADA_RANK), lambda l, j: (l, 0, 0)),
            pl.BlockSpec((1, ADA_RANK, D_MODEL), lambda l, j: (l, 0, j)),
            pl.BlockSpec((1, 1, D_MODEL), lambda l, j: (l, 0, j)),
        ],
        out_specs=pl.BlockSpec((1, COND_ROWS, D_MODEL), lambda l, j: (l, 0, j)),
        out_shape=jax.ShapeDtypeStruct((DEPTH, COND_ROWS, N_MOD * D_MODEL), F32),
        compiler_params=_params("arbitrary", "arbitrary"),
        name="ada_mods",
    )(cond, ada_down, ada_up, ada_bias[:, None, :])


def _norm_mod_tile(h_ref, g_ref, shift_ref, scale_ref):
    x = h_ref[...]
    ms = jnp.mean(x * x, axis=-1, keepdims=True)
    y = x * lax.rsqrt(ms + EPS) * g_ref[...]
    return y * (1.0 + scale_ref[0, 0]) + shift_ref[0, 0]


def _norm_mod_kernel(h_ref, g_ref, shift_ref, scale_ref, o_ref):
    o_ref[...] = _norm_mod_tile(h_ref, g_ref, shift_ref, scale_ref).astype(BF16)


def _split_bf16(x):
    hi = x.astype(BF16)
    lo = (x - hi.astype(F32)).astype(BF16)
    return hi, lo


def _first_index_of_max(v, idx, n):
    m = jnp.max(v, axis=0, keepdims=True)
    i = jnp.min(jnp.where(v == m, idx, n), axis=0, keepdims=True)
    return m, i


def _norm_route_kernel(h_ref, g_ref, shift_ref, scale_ref, wr_ref, eb_ref,
                       o_ref, xw_ref, idx_ref, wts_ref, rank_ref, cnt_ref, run_ref):
    @pl.when(pl.program_id(0) == 0)
    def _():
        run_ref[...] = jnp.zeros_like(run_ref)

    xn = _norm_mod_tile(h_ref, g_ref, shift_ref, scale_ref)
    o_ref[...] = xn.astype(BF16)
    _store_slabs(xw_ref, _pack_halves(xn[:, :HALF_D], xn[:, HALF_D:]))
    x_hi, x_lo = _split_bf16(xn)
    w_hi, w_lo = _split_bf16(wr_ref[...])
    logits = _dot(x_hi, w_hi) + _dot(x_lo, w_hi) + _dot(x_hi, w_lo)
    lt = logits.T[:N_EXPERTS]
    scores = 1.0 / (1.0 + jnp.exp(-lt))
    sel = scores + eb_ref[...]
    n_tok = sel.shape[1]
    sub = lax.broadcasted_iota(I32, (GROUP_SIZE, n_tok), 0).astype(F32)
    grp_scores = []
    for g in range(N_EXPERT_GROUPS):
        sg = sel[g * GROUP_SIZE:(g + 1) * GROUP_SIZE]
        m1, i1 = _first_index_of_max(sg, sub, float(GROUP_SIZE))
        m2 = jnp.max(jnp.where(sub == i1, -jnp.inf, sg), axis=0, keepdims=True)
        grp_scores.append(m1 + m2)
    eidx_i = lax.broadcasted_iota(I32, sel.shape, 0)
    eidx = eidx_i.astype(F32)
    egrp = (eidx_i // GROUP_SIZE).astype(F32)
    best = jnp.full((1, n_tok), -jnp.inf, F32)
    for g in range(N_EXPERT_GROUPS):
        best = jnp.maximum(best, grp_scores[g])
    g1 = jnp.full((1, n_tok), float(N_EXPERT_GROUPS), F32)
    for g in reversed(range(N_EXPERT_GROUPS)):
        g1 = jnp.where(grp_scores[g] == best, float(g), g1)
    second = jnp.full((1, n_tok), -jnp.inf, F32)
    for g in range(N_EXPERT_GROUPS):
        second = jnp.maximum(second, jnp.where(g1 == float(g), -jnp.inf, grp_scores[g]))
    g2 = jnp.full((1, n_tok), float(N_EXPERT_GROUPS), F32)
    for g in reversed(range(N_EXPERT_GROUPS)):
        g2 = jnp.where((grp_scores[g] == second) & (g1 != float(g)), float(g), g2)
    masked = jnp.where((egrp == g1) | (egrp == g2), sel, -jnp.inf)
    picks, weights, hits = [], [], []
    for _ in range(TOP_K):
        _, ik = _first_index_of_max(masked, eidx, float(N_EXPERTS))
        hit = eidx == ik
        weights.append(jnp.sum(jnp.where(hit, scores, 0.0), axis=0, keepdims=True))
        picks.append(ik)
        hits.append(hit)
        masked = jnp.where(hit, -jnp.inf, masked)
    w = jnp.concatenate(weights, axis=0)
    idx_ref[...] = jnp.concatenate(picks, axis=0).astype(I32)
    wts_ref[...] = w / jnp.sum(w, axis=0, keepdims=True) * ROUTED_SCALE
    member = jnp.zeros(sel.shape, F32)
    for hit in hits:
        member = jnp.where(hit, 1.0, member)
    earlier = lax.broadcasted_iota(I32, (n_tok, n_tok), 0) < lax.broadcasted_iota(I32, (n_tok, n_tok), 1)
    before = run_ref[...] + _dot(member.astype(BF16), jnp.where(earlier, 1.0, 0.0).astype(BF16))
    rank_ref[...] = jnp.concatenate(
        [jnp.sum(jnp.where(hit, before, 0.0), axis=0, keepdims=True) for hit in hits], axis=0).astype(I32)
    run_ref[...] = run_ref[...] + jnp.sum(member, axis=1, keepdims=True)
    cnt_ref[...] = run_ref[...]


def _seg_of_tile(i, tm, seq, n_batch):
    return jnp.minimum((i * tm) // seq, n_batch)


def _norm_mod(h, g, mods, shift_k, scale_k, rows, seq, n_batch, router=None):
    tm = TOKEN_TILE
    seg = functools.partial(_seg_of_tile, tm=tm, seq=seq, n_batch=n_batch)
    in_specs = [
        pl.BlockSpec((tm, D_MODEL), lambda i: (i, 0)),
        pl.BlockSpec((1, D_MODEL), lambda i: (0, 0)),
        pl.BlockSpec((1, 1, 1, D_MODEL), lambda i: (seg(i), shift_k, 0, 0)),
        pl.BlockSpec((1, 1, 1, D_MODEL), lambda i: (seg(i), scale_k, 0, 0)),
    ]
    xn_spec = pl.BlockSpec((tm, D_MODEL), lambda i: (i, 0))
    xn_shape = jax.ShapeDtypeStruct((rows, D_MODEL), BF16)
    if router is None:
        return pl.pallas_call(
            _norm_mod_kernel, grid=(rows // tm,), in_specs=in_specs, out_specs=xn_spec, out_shape=xn_shape,
            compiler_params=_params("arbitrary"), name="norm_mod",
        )(h, g.reshape(1, D_MODEL), mods, mods)
    w_router, e_bias = router
    w_pad = jnp.pad(w_router, ((0, 0), (0, 128 - N_EXPERTS)))
    in_specs += [
        pl.BlockSpec((D_MODEL, 128), lambda i: (0, 0)),
        pl.BlockSpec((N_EXPERTS, 1), lambda i: (0, 0)),
    ]
    per_tok = pl.BlockSpec((TOP_K, tm), lambda i: (0, i))
    return pl.pallas_call(
        _norm_route_kernel, grid=(rows // tm,), in_specs=in_specs,
        out_specs=[xn_spec, pl.BlockSpec((tm * SLAB, LANES), lambda i: (i, 0)), per_tok, per_tok, per_tok,
                   pl.BlockSpec((N_EXPERTS, 1), lambda i: (0, 0))],
        out_shape=[xn_shape, jax.ShapeDtypeStruct((rows * SLAB, LANES), U32),
                   jax.ShapeDtypeStruct((TOP_K, rows), I32), jax.ShapeDtypeStruct((TOP_K, rows), F32),
                   jax.ShapeDtypeStruct((TOP_K, rows), I32), jax.ShapeDtypeStruct((N_EXPERTS, 1), F32)],
        scratch_shapes=[pltpu.VMEM((N_EXPERTS, 1), F32)],
        compiler_params=_params("arbitrary"), name="norm_route",
    )(h, g.reshape(1, D_MODEL), mods, mods, w_pad, e_bias.reshape(N_EXPERTS, 1))


def _cast_weight_once(w_ref, w_scr):
    @pl.when(pl.program_id(1) == 0)
    def _():
        w_scr[...] = w_ref[0].astype(BF16)


def _qkv_kernel(a_ref, w_ref, gain_ref, cos_ref, sin_ref, o_ref, w_scr, *, n_qk_tiles, rope):
    _cast_weight_once(w_ref, w_scr)
    j = pl.program_id(0)
    acc = _dot(a_ref[...], w_scr[...])
    n_heads = acc.shape[1] // HEAD_DIM

    @pl.when(j < n_qk_tiles)
    def _():
        lane = lax.broadcasted_iota(I32, (acc.shape[0], HEAD_DIM), 1)
        quarter = ROPE_AXIS_DIM // 2
        first_half = (lane % ROPE_AXIS_DIM) < quarter
        outs = []
        for hh in range(n_heads):
            x = acc[:, hh * HEAD_DIM:(hh + 1) * HEAD_DIM]
            ms = jnp.mean(x * x, axis=-1, keepdims=True)
            y = x * lax.rsqrt(ms + EPS) * gain_ref[:, hh * HEAD_DIM:(hh + 1) * HEAD_DIM]
            if rope:
                partner = jnp.where(first_half, pltpu.roll(y, HEAD_DIM - quarter, 1), pltpu.roll(y, quarter, 1))
                y = y * cos_ref[...] + partner * sin_ref[...]
            outs.append(y.astype(BF16))
        o_ref[...] = jnp.concatenate(outs, axis=1)

    @pl.when(j >= n_qk_tiles)
    def _():
        o_ref[...] = acc.astype(BF16)


def _qkv_proj(xn, w_stack, w_slot, gain_row, cos, sin, n_qk_cols, rope):
    rows, width = xn.shape[0], w_stack.shape[2]
    tm, tn = ROW_TILE, COL_TILE
    return pl.pallas_call(
        functools.partial(_qkv_kernel, n_qk_tiles=n_qk_cols // tn, rope=rope),
        grid=(width // tn, rows // tm),
        in_specs=[
            pl.BlockSpec((tm, D_MODEL), lambda j, i: (i, 0)),
            pl.BlockSpec((1, D_MODEL, tn), lambda j, i: (w_slot, 0, j)),
            pl.BlockSpec((1, tn), lambda j, i: (0, j)),
            pl.BlockSpec((tm, HEAD_DIM), lambda j, i: (i, 0)),
            pl.BlockSpec((tm, HEAD_DIM), lambda j, i: (i, 0)),
        ],
        out_specs=pl.BlockSpec((tm, tn), lambda j, i: (i, j)),
        out_shape=jax.ShapeDtypeStruct((rows, width), BF16),
        scratch_shapes=[pltpu.VMEM((D_MODEL, tn), BF16)],
        compiler_params=_params("arbitrary", "arbitrary"),
        name="qkv_proj",
    )(xn, w_stack, gain_row, cos, sin)


def _mm_res_kernel(a_ref, w_ref, h_ref, gate_ref, o_ref, w_scr):
    _cast_weight_once(w_ref, w_scr)
    o_ref[...] = h_ref[...] + gate_ref[0, 0] * _dot(a_ref[...], w_scr[...])


def _mm_residual(a, w_stack, w_slot, h, mods, gate_k, rows, seq, n_batch):
    _, k_dim, width = w_stack.shape
    tm, tn = ROW_TILE, COL_TILE
    seg = functools.partial(_seg_of_tile, tm=tm, seq=seq, n_batch=n_batch)
    return pl.pallas_call(
        _mm_res_kernel,
        grid=(width // tn, rows // tm),
        in_specs=[
            pl.BlockSpec((tm, k_dim), lambda j, i: (i, 0)),
            pl.BlockSpec((1, k_dim, tn), lambda j, i: (w_slot, 0, j)),
            pl.BlockSpec((tm, tn), lambda j, i: (i, j)),
            pl.BlockSpec((1, 1, 1, tn), lambda j, i: (seg(i), gate_k, 0, j)),
        ],
        out_specs=pl.BlockSpec((tm, tn), lambda j, i: (i, j)),
        out_shape=jax.ShapeDtypeStruct((rows, width), F32),
        scratch_shapes=[pltpu.VMEM((k_dim, tn), BF16)],
        compiler_params=_params("arbitrary", "arbitrary"),
        name="mm_residual",
    )(a, w_stack, h, mods)


def _attn_kernel(*refs, n_local, group, block, kind, has_sink, has_prev, n_blocks, heads):
    refs = list(refs)
    sink_ref = refs.pop(0) if has_sink else None
    q_ref = refs.pop(0)
    k_loc = [refs.pop(0) for _ in range(n_local)]
    v_loc = [refs.pop(0) for _ in range(n_local)]
    kc_ref, vc_ref = refs.pop(0), refs.pop(0)
    bias_ref = refs.pop(0) if kind == "nat" else None
    if has_prev:
        refs.pop(0)
    o_ref = refs.pop(0)
    head0 = pl.program_id(1) * heads
    i = pl.program_id(2)
    scale = HEAD_DIM ** -0.5
    n_rows = group * block

    valid = None
    if n_local:
        row = lax.broadcasted_iota(I32, (n_rows, 1), 0)
        col = lax.broadcasted_iota(I32, (1, n_local * block), 1)
        if kind == "swa":
            rel = col - block - (row % block)
            lo = jnp.where(i == 0, block, 0)
            hi = jnp.where(i == n_blocks - 1, 2 * block, 3 * block)
            valid = (rel >= -WINDOW) & (rel <= WINDOW) & (col >= lo) & (col < hi)
        else:
            q_row = NAT_ROWS_PER_BLOCK * i + row // GRID_W
            k_row = NAT_ROWS_PER_BLOCK * (i - 1) + col // GRID_W
            r0 = jnp.clip(q_row - NAT_KH // 2, 0, n_blocks * NAT_ROWS_PER_BLOCK - NAT_KH)
            valid = (k_row >= r0) & (k_row < r0 + NAT_KH)
    g_of_row = lax.broadcasted_iota(I32, (n_rows, 1), 0) // block if has_sink else None

    outs = []
    for hd in range(heads):
        kv_cols = slice(hd * HEAD_DIM, (hd + 1) * HEAD_DIM)
        if group > 1:
            q = jnp.concatenate([q_ref[:, (hd * group + g) * HEAD_DIM:(hd * group + g + 1) * HEAD_DIM]
                                 for g in range(group)], axis=0)
        else:
            q = q_ref[:, kv_cols]
        s_ctx = _dot_nt(q, kc_ref[:, kv_cols]) * scale
        m = jnp.max(s_ctx, axis=-1, keepdims=True)
        if n_local:
            k = jnp.concatenate([r[:, kv_cols] for r in k_loc], axis=0)
            v = jnp.concatenate([r[:, kv_cols] for r in v_loc], axis=0)
            s_loc = _dot_nt(q, k) * scale
            if kind == "nat":
                s_loc = s_loc + bias_ref[hd]
            s_loc = jnp.where(valid, s_loc, MASKED)
            m = jnp.maximum(m, jnp.max(s_loc, axis=-1, keepdims=True))
        if has_sink:
            sink = jnp.zeros((n_rows, 1), F32)
            for g in range(group):
                sink = jnp.where(g_of_row == g, sink_ref[(head0 + hd) * group + g], sink)
            m = jnp.maximum(m, sink)
        p_ctx = jnp.exp(s_ctx - m)
        denom = jnp.sum(p_ctx, axis=-1, keepdims=True)
        o = _dot(p_ctx.astype(BF16), vc_ref[:, kv_cols])
        if n_local:
            p_loc = jnp.exp(s_loc - m)
            denom = denom + jnp.sum(p_loc, axis=-1, keepdims=True)
            o = o + _dot(p_loc.astype(BF16), v)
        if has_sink:
            denom = denom + jnp.exp(sink - m)
        o = (o * (1.0 / denom)).astype(BF16)
        outs += [o[g * block:(g + 1) * block] for g in range(group)]
    o_ref[...] = jnp.concatenate(outs, axis=1) if len(outs) > 1 else outs[0]


def _attention(qkv, *, kind, latent, n_batch, seq, ctx_len, k_col0, v_col0, group, out_rows,
               sink=None, bias=None, prev=None):
    heads = ATTN_HEADS_PER_STEP
    n_kv = v_col0 - k_col0
    assert n_kv % heads == 0 and k_col0 % heads == 0 and v_col0 % heads == 0
    n_steps_h = n_kv // heads
    kc0, vc0 = k_col0 // heads, v_col0 // heads
    block = (WINDOW if kind == "swa" else NAT_BLOCK) if latent else ctx_len
    n_blocks = seq // block if latent else 1
    n_local = 3 if latent else 0
    ctx_blk0 = n_batch * seq // ctx_len
    q_blk0 = 0 if latent else n_batch * seq // block
    has_sink = sink is not None
    nat_order = kind == "nat"

    def ids(a, b, c):
        return (b, a, c) if nat_order else (a, b, c)

    def q_map(a, b, c):
        bb, hh, ii = ids(a, b, c)
        return (q_blk0 + bb * n_blocks + ii, hh)

    def loc_map(delta, col0):
        def f(a, b, c):
            bb, hh, ii = ids(a, b, c)
            return (bb * n_blocks + jnp.clip(ii + delta, 0, n_blocks - 1), col0 + hh)
        return f

    def ctx_map(col0):
        def f(a, b, c):
            bb, hh, _ = ids(a, b, c)
            return (ctx_blk0 + bb, col0 + hh)
        return f

    in_specs, args = [], []
    if has_sink:
        in_specs.append(pl.BlockSpec(memory_space=pltpu.SMEM))
        args.append(sink.astype(F32))
    in_specs.append(pl.BlockSpec((block, heads * group * HEAD_DIM), q_map))
    args.append(qkv)
    for col0 in ((kc0, vc0) if latent else ()):
        for delta in (-1, 0, 1):
            in_specs.append(pl.BlockSpec((block, heads * HEAD_DIM), loc_map(delta, col0)))
            args.append(qkv)
    for col0 in (kc0, vc0):
        in_specs.append(pl.BlockSpec((ctx_len, heads * HEAD_DIM), ctx_map(col0)))
        args.append(qkv)
    use_bias = nat_order and latent
    if use_bias:
        in_specs.append(pl.BlockSpec((heads, block, 3 * block), lambda a, b, c: (a, 0, 0)))
        args.append(bias)
    aliases = {}
    if prev is not None:
        aliases = {len(args): 0}
        in_specs.append(pl.BlockSpec(memory_space=pl.ANY))
        args.append(prev)
    grid = (n_steps_h, n_batch, n_blocks) if nat_order else (n_batch, n_steps_h, n_blocks)
    kern = functools.partial(_attn_kernel, n_local=n_local, group=group, block=block,
                             kind=("nat" if use_bias else kind if latent else "ctx"),
                             has_sink=has_sink, has_prev=prev is not None, n_blocks=n_blocks, heads=heads)
    return pl.pallas_call(
        kern, grid=grid, in_specs=in_specs,
        out_specs=pl.BlockSpec((block, heads * group * HEAD_DIM), q_map),
        out_shape=jax.ShapeDtypeStruct((out_rows, n_kv * group * HEAD_DIM), BF16),
        input_output_aliases=aliases,
        compiler_params=_params("arbitrary", "arbitrary", "arbitrary"),
        name=f"attn_{kind}_{'latent' if latent else 'ctx'}",
    )(*args)


def _nat_bias_table(rpb):
    q_c = np.arange(GRID_W)[:, None]
    k_c = np.arange(GRID_W)[None, :]
    d_col = np.clip(k_c - q_c + NAT_KW - 1, 0, 2 * NAT_KW - 2)
    c0 = np.clip(q_c - NAT_KW // 2, 0, GRID_W - NAT_KW)
    col_valid = (k_c >= c0) & (k_c < c0 + NAT_KW)
    onehot = (np.arange(2 * NAT_KW - 1)[:, None, None] == d_col[None]).astype(np.float32)
    by_col = jnp.einsum("hrd,dqk->hrqk", rpb.astype(F32), jnp.asarray(onehot), precision=lax.Precision.HIGHEST)
    by_col = jnp.where(col_valid[None, None], by_col, MASKED)
    n_key_rows = 3 * NAT_ROWS_PER_BLOCK
    rows = []
    for q_r in range(NAT_ROWS_PER_BLOCK):
        d_row = [k_r - NAT_ROWS_PER_BLOCK - q_r + NAT_KH - 1 for k_r in range(n_key_rows)]
        rows.append(jnp.concatenate([by_col[:, d] for d in d_row], axis=-1))
    return jnp.concatenate(rows, axis=1)


def _dft_chan_kernel(a_ref, w_ref, o_ref):
    o_ref[0] = _dot(a_ref[...], w_ref[0]).astype(BF16)


def _dft_pos_kernel(cn_ref, sn_ref, xc_ref, xs_ref, *rest, scale):
    o_ref = rest[-1]
    o_ref[...] = ((_dot(cn_ref[...], xc_ref[0]) - _dot(sn_ref[...], xs_ref[0])) * scale).astype(BF16)


def _dft_tables(n):
    k = jnp.arange(n, dtype=I32)
    ang = ((k[:, None] * k[None, :]) % n).astype(F32) * (2.0 * np.pi / n)
    return jnp.cos(ang).astype(BF16), jnp.sin(ang).astype(BF16)


def _fourier_mix(xn, rows, n_batch, seq, ctx_len, ctx_out):
    gd = FOURIER_GROUP_DIM
    cd, sd = _dft_tables(gd)
    chan = jnp.stack([cd, sd])
    tm = ROW_TILE
    xcs = pl.pallas_call(
        _dft_chan_kernel,
        grid=(rows // tm, FOURIER_GROUPS, 2),
        in_specs=[pl.BlockSpec((tm, gd), lambda i, g, t: (i, g)),
                  pl.BlockSpec((1, gd, gd), lambda i, g, t: (t, 0, 0))],
        out_specs=pl.BlockSpec((1, tm, gd), lambda i, g, t: (t, i, g)),
        out_shape=jax.ShapeDtypeStruct((2, rows, D_MODEL), BF16),
        compiler_params=_params("arbitrary", "arbitrary", "arbitrary"),
        name="dft_channels",
    )(xn, chan)

    def pos_call(n, tmo, row_blk0, in_blk0, out_rows, prev):
        cn, sn = _dft_tables(n)
        tn = COL_TILE
        per = n // tmo
        in_specs = [
            pl.BlockSpec((tmo, n), lambda b, i, j: (i, 0)),
            pl.BlockSpec((tmo, n), lambda b, i, j: (i, 0)),
            pl.BlockSpec((1, n, tn), lambda b, i, j: (0, in_blk0 + b, j)),
            pl.BlockSpec((1, n, tn), lambda b, i, j: (1, in_blk0 + b, j)),
        ]
        args = [cn, sn, xcs, xcs]
        aliases = {}
        if prev is not None:
            in_specs.append(pl.BlockSpec(memory_space=pl.ANY))
            args.append(prev)
            aliases = {4: 0}
        return pl.pallas_call(
            functools.partial(_dft_pos_kernel, scale=float((n * gd) ** -0.5)),
            grid=(n_batch, per, D_MODEL // tn),
            in_specs=in_specs,
            out_specs=pl.BlockSpec((tmo, tn), lambda b, i, j: (row_blk0 + b * per + i, j)),
            out_shape=jax.ShapeDtypeStruct((out_rows, D_MODEL), BF16),
            input_output_aliases=aliases,
            compiler_params=_params("arbitrary", "arbitrary", "arbitrary"),
            name=f"dft_positions_{n}",
        )(*args)

    z = pos_call(seq, 512, 0, 0, rows, None)
    if ctx_out:
        z = pos_call(ctx_len, ctx_len, n_batch * seq // ctx_len, n_batch * seq // ctx_len, rows, z)
    return z


def _slab_copy(src_ref, src_token, dst_ref, dst_token, sem):
    src = src_ref.at[pl.ds(pl.multiple_of(src_token * SLAB, SLAB), SLAB)]
    dst = dst_ref.at[pl.ds(pl.multiple_of(dst_token * SLAB, SLAB), SLAB)]
    return pltpu.make_async_copy(src, dst, sem)


def _load_slots(pos_ref, pos_smem, sem):
    cp = pltpu.make_async_copy(pos_ref.at[0, 0], pos_smem, sem)
    cp.start()
    cp.wait()


def _for_token_groups(n_tokens, fn):
    def group(g, c):
        for k in range(TOP_K):
            for u in range(DMA_UNROLL):
                fn(k, g * DMA_UNROLL + u)
        return c
    lax.fori_loop(0, n_tokens // DMA_UNROLL, group, 0)


def _dispatch_step(i, n_steps, cnt_ref, off_ref, pos_ref, xw_ref, xs_ref,
                   pos_smem, zero_ref, xbuf_ref, sem_idx, sem_load, sem_rows):
    tm = pos_smem.shape[0] // TOP_K

    @pl.when(i == 0)
    def _():
        zero_ref[...] = jnp.zeros_like(zero_ref)

        def per_expert(e, carry):
            count = cnt_ref[e]
            first = off_ref[e] + count
            n_pad = (count + EXPERT_TILE - 1) // EXPERT_TILE * EXPERT_TILE - count

            def start(r, c):
                _slab_copy(zero_ref, 0, xs_ref, first + r, sem_rows.at[0]).start()
                return c

            def wait(r, c):
                _slab_copy(zero_ref, 0, xs_ref, first + r, sem_rows.at[0]).wait()
                return c

            lax.fori_loop(0, n_pad, start, 0)
            lax.fori_loop(0, n_pad, wait, 0)
            return carry

        lax.fori_loop(0, N_EXPERTS, per_expert, 0)

    def load(step):
        which = step % N_DISPATCH_BUFS
        src = xw_ref.at[pl.ds(pl.multiple_of(step * tm * SLAB, tm * SLAB), tm * SLAB)]
        return pltpu.make_async_copy(src, xbuf_ref.at[which], sem_load.at[which])

    def drain(step):
        which = step % N_DISPATCH_BUFS
        _for_token_groups(tm, lambda k, t: _slab_copy(xbuf_ref.at[which], t, xs_ref, 0, sem_rows.at[which]).wait())

    @pl.when(i == 0)
    def _():
        load(i).start()

    @pl.when(i >= N_DISPATCH_BUFS - 1)
    def _():
        drain(i - (N_DISPATCH_BUFS - 1))

    @pl.when(i + 1 < n_steps)
    def _():
        load(i + 1).start()

    load(i).wait()
    _load_slots(pos_ref, pos_smem, sem_idx)
    cur = i % N_DISPATCH_BUFS
    _for_token_groups(tm, lambda k, t: _slab_copy(
        xbuf_ref.at[cur], t, xs_ref, pos_smem[k * tm + t], sem_rows.at[cur]).start())

    @pl.when(i == n_steps - 1)
    def _():
        for back in reversed(range(N_DISPATCH_BUFS - 1)):
            @pl.when(i - back >= 0)
            def _(back=back):
                drain(i - back)


def _swiglu_dispatch_kernel(cnt_ref, off_ref, a_ref, wg_ref, wu_ref, pos_ref, xw_ref, o_ref, xs_ref,
                            wg_scr, wu_scr, *dispatch_scratch):
    n_inner = pl.num_programs(1)
    step = pl.program_id(0) * n_inner + pl.program_id(1)
    _dispatch_step(step, pl.num_programs(0) * n_inner, cnt_ref, off_ref, pos_ref, xw_ref, xs_ref, *dispatch_scratch)
    _cast_weight_once(wg_ref, wg_scr)
    _cast_weight_once(wu_ref, wu_scr)
    a = a_ref[...]
    o_ref[...] = (_silu(_dot(a, wg_scr[...])) * _dot(a, wu_scr[...])).astype(BF16)


def _swiglu_and_dispatch(a, wg_stack, wu_stack, layer, rows, xw, pos_tiles, counts, offs, n_slots):
    width = wg_stack.shape[2]
    tm, tn, tok = ROW_TILE, COL_TILE // 2, TOKEN_TILE
    n_inner = rows // tm
    assert (width // tn) * n_inner == rows // tok
    w_spec = pl.BlockSpec((1, D_MODEL, tn), lambda j, i, cnt, off: (layer, 0, j))
    grid_spec = pltpu.PrefetchScalarGridSpec(
        num_scalar_prefetch=2,
        grid=(width // tn, n_inner),
        in_specs=[
            pl.BlockSpec((tm, D_MODEL), lambda j, i, cnt, off: (i, 0)),
            w_spec, w_spec,
            pl.BlockSpec((1, 1, TOP_K * tok), lambda j, i, cnt, off: (j * n_inner + i, 0, 0)),
            pl.BlockSpec(memory_space=pl.ANY),
        ],
        out_specs=[pl.BlockSpec((tm, tn), lambda j, i, cnt, off: (i, j)), pl.BlockSpec(memory_space=pl.ANY)],
        scratch_shapes=[
            pltpu.VMEM((D_MODEL, tn), BF16),
            pltpu.VMEM((D_MODEL, tn), BF16),
            pltpu.SMEM((TOP_K * tok,), I32),
            pltpu.VMEM((SLAB, LANES), U32),
            pltpu.VMEM((N_DISPATCH_BUFS, tok * SLAB, LANES), U32),
            pltpu.SemaphoreType.DMA,
            pltpu.SemaphoreType.DMA((N_DISPATCH_BUFS,)),
            pltpu.SemaphoreType.DMA((N_DISPATCH_BUFS,)),
        ],
    )
    return pl.pallas_call(
        _swiglu_dispatch_kernel, grid_spec=grid_spec,
        out_shape=[jax.ShapeDtypeStruct((rows, width), BF16), jax.ShapeDtypeStruct((n_slots * SLAB, LANES), U32)],
        compiler_params=_params("arbitrary", "arbitrary", disable_bounds_checks=True),
        name="shared_swiglu_dispatch",
    )(counts, offs, a, wg_stack, wu_stack, pos_tiles, xw)


def _expert_kernel(te_ref, nu_ref, x_ref, wg_ref, wu_ref, wd_ref, o_ref, wgu_scr, wd_scr):
    j = pl.program_id(0)

    @pl.when(j < nu_ref[0])
    def _():
        @pl.when((j == 0) | (te_ref[j] != te_ref[jnp.maximum(j - 1, 0)]))
        def _():
            wgu_scr[:, :EXPERT_HIDDEN] = wg_ref[0, 0].astype(BF16)
            wgu_scr[:, EXPERT_HIDDEN:] = wu_ref[0, 0].astype(BF16)
            wd_scr[...] = wd_ref[0, 0].astype(BF16)

        x_lo, x_hi = _unpack_halves(_load_slabs(x_ref, EXPERT_TILE))
        x_lo, x_hi = x_lo.astype(BF16), x_hi.astype(BF16)
        gate_up = _dot(x_lo, wgu_scr[:HALF_D]) + _dot(x_hi, wgu_scr[HALF_D:])
        hid = (_silu(gate_up[:, :EXPERT_HIDDEN]) * gate_up[:, EXPERT_HIDDEN:]).astype(BF16)
        _store_slabs(o_ref, _pack_halves(_dot(hid, wd_scr[:, :HALF_D]), _dot(hid, wd_scr[:, HALF_D:])))


def _experts(x_sorted, tile_expert, n_used, wg, wu, wd, layer):
    n_rows = x_sorted.shape[0] // SLAB
    tm = EXPERT_TILE

    def row_map(j, te, nu):
        return (jnp.minimum(j, nu[0] - 1), 0)

    def w_map(j, te, nu):
        return (layer, te[j], 0, 0)

    grid_spec = pltpu.PrefetchScalarGridSpec(
        num_scalar_prefetch=2,
        grid=(n_rows // tm,),
        in_specs=[
            pl.BlockSpec((tm * SLAB, LANES), row_map),
            pl.BlockSpec((1, 1, D_MODEL, EXPERT_HIDDEN), w_map),
            pl.BlockSpec((1, 1, D_MODEL, EXPERT_HIDDEN), w_map),
            pl.BlockSpec((1, 1, EXPERT_HIDDEN, D_MODEL), w_map),
        ],
        out_specs=pl.BlockSpec((tm * SLAB, LANES), row_map),
        scratch_shapes=[
            pltpu.VMEM((D_MODEL, 2 * EXPERT_HIDDEN), BF16),
            pltpu.VMEM((EXPERT_HIDDEN, D_MODEL), BF16),
        ],
    )
    return pl.pallas_call(
        _expert_kernel, grid_spec=grid_spec,
        out_shape=jax.ShapeDtypeStruct((n_rows * SLAB, LANES), U32),
        compiler_params=_params("arbitrary"),
        name="routed_experts",
    )(tile_expert, n_used, x_sorted, wg, wu, wd)


def _combine_kernel(pos_ref, nxt_ref, h_ref, w_ref, gate_ref, y_ref, o_ref, pos_smem, buf_ref, sem_idx, sem_rows):
    i = pl.program_id(0)
    n_steps = pl.num_programs(0)
    tm = h_ref.shape[0]
    cur = i % 2

    def gather(slot_ref, which):
        _load_slots(slot_ref, pos_smem, sem_idx)
        _for_token_groups(tm, lambda k, t: _slab_copy(
            y_ref, pos_smem[k * tm + t], buf_ref.at[which, k], t, sem_rows.at[which]).start())

    @pl.when(i == 0)
    def _():
        gather(pos_ref, 0)

    @pl.when(i + 1 < n_steps)
    def _():
        gather(nxt_ref, 1 - cur)

    _for_token_groups(tm, lambda k, t: _slab_copy(y_ref, 0, buf_ref.at[cur, k], t, sem_rows.at[cur]).wait())
    sub = 8

    def rows_chunk(r, c):
        first = pl.multiple_of(r * sub, sub)
        rs = pl.ds(first, sub)
        w = w_ref[rs, :]
        wk = [jnp.broadcast_to(w[:, k:k + 1], (sub, LANES)) for k in range(TOP_K)]
        for part in range(SLAB):
            terms = []
            for k in range(TOP_K):
                y_lo, y_hi = _unpack_halves(buf_ref[cur, k, _slab_rows(first, sub, part), :])
                terms.append((wk[k] * y_lo, wk[k] * y_hi))
            acc_lo = (terms[0][0] + terms[1][0]) + (terms[2][0] + terms[3][0])
            acc_hi = (terms[0][1] + terms[1][1]) + (terms[2][1] + terms[3][1])
            lo = pl.ds(part * LANES, LANES)
            hi = pl.ds(HALF_D + part * LANES, LANES)
            o_ref[rs, lo] = h_ref[rs, lo] + gate_ref[0, 0, :, lo] * acc_lo
            o_ref[rs, hi] = h_ref[rs, hi] + gate_ref[0, 0, :, hi] * acc_hi
        return c

    lax.fori_loop(0, tm // sub, rows_chunk, 0)


def _combine(h, y_sorted, pos_tiles, wts_t, mods, gate_k, rows, seq, n_batch):
    tm = TOKEN_TILE
    n_steps = rows // tm
    seg = functools.partial(_seg_of_tile, tm=tm, seq=seq, n_batch=n_batch)
    return pl.pallas_call(
        _combine_kernel,
        grid=(n_steps,),
        in_specs=[
            pl.BlockSpec((1, 1, TOP_K * tm), lambda i: (i, 0, 0)),
            pl.BlockSpec((1, 1, TOP_K * tm), lambda i: (jnp.minimum(i + 1, n_steps - 1), 0, 0)),
            pl.BlockSpec((tm, D_MODEL), lambda i: (i, 0)),
            pl.BlockSpec((tm, TOP_K), lambda i: (i, 0)),
            pl.BlockSpec((1, 1, 1, D_MODEL), lambda i: (seg(i), gate_k, 0, 0)),
            pl.BlockSpec(memory_space=pl.ANY),
        ],
        out_specs=pl.BlockSpec((tm, D_MODEL), lambda i: (i, 0)),
        out_shape=jax.ShapeDtypeStruct((rows, D_MODEL), F32),
        scratch_shapes=[
            pltpu.SMEM((TOP_K * tm,), I32),
            pltpu.VMEM((2, TOP_K, tm * SLAB, LANES), U32),
            pltpu.SemaphoreType.DMA,
            pltpu.SemaphoreType.DMA((2,)),
        ],
        compiler_params=_params("arbitrary", disable_bounds_checks=True),
        name="moe_combine",
    )(pos_tiles, pos_tiles, h, wts_t, mods, y_sorted)


def _dispatch_plan(idx, rank, counts_f, rows):
    tm = EXPERT_TILE
    counts = counts_f[:, 0].astype(I32)
    padded = (counts + tm - 1) // tm * tm
    ends = jnp.cumsum(padded)
    offs = ends - padded
    expert = jnp.arange(N_EXPERTS, dtype=I32)[:, None, None]
    pos = rank + jnp.sum(jnp.where(idx[None] == expert, offs[:, None, None], 0), axis=0)
    n_tok_tiles = rows // TOKEN_TILE
    pos_tiles = pos.reshape(TOP_K, n_tok_tiles, TOKEN_TILE).transpose(1, 0, 2).reshape(n_tok_tiles, 1, TOP_K * TOKEN_TILE)
    n_slots = TOP_K * rows + N_EXPERTS * tm
    n_used = ends[-1] // tm
    tile_start = jnp.arange(n_slots // tm, dtype=I32) * tm
    tile_expert = jnp.sum((tile_start[:, None] >= ends[None, :]).astype(I32), axis=1)
    last_expert = jnp.sum((((n_used - 1) * tm) >= ends).astype(I32))
    tile_expert = jnp.where(tile_start < ends[-1], tile_expert, last_expert).astype(I32)
    return pos_tiles, counts, offs.astype(I32), tile_expert, n_used.astype(I32).reshape(1), n_slots


def _moe(h, g2, mods, layer, rows, seq, n_batch, w_router, e_bias, wg, wu, wd, sg, su, sd):
    xn, xw, idx, wts, rank, counts_f = _norm_mod(h, g2, mods, 3, 4, rows, seq, n_batch, router=(w_router, e_bias))
    pos_tiles, counts, offs, tile_expert, n_used, n_slots = _dispatch_plan(idx, rank, counts_f, rows)
    hid_s, x_sorted = _swiglu_and_dispatch(xn, sg, su, layer, rows, xw, pos_tiles, counts, offs, n_slots)
    h = _mm_residual(hid_s, sd, layer, h, mods, 5, rows, seq, n_batch)
    y_sorted = _experts(x_sorted, tile_expert, n_used, wg, wu, wd, layer)
    return _combine(h, y_sorted, pos_tiles, wts.T, mods, 5, rows, seq, n_batch)


def _rope_tables(n_batch, seq, total_rows):
    pos = jnp.arange(seq)
    row = (pos // GRID_W).astype(F32)
    col = (pos % GRID_W).astype(F32)
    inv = ROPE_THETA ** (-jnp.arange(0, ROPE_AXIS_DIM, 2, dtype=F32) / ROPE_AXIS_DIM)
    ar = row[:, None] * inv
    ac = col[:, None] * inv
    cos = jnp.concatenate([jnp.cos(ar), jnp.cos(ar), jnp.cos(ac), jnp.cos(ac)], axis=-1)
    sin = jnp.concatenate([-jnp.sin(ar), jnp.sin(ar), -jnp.sin(ac), jnp.sin(ac)], axis=-1)
    n_ctx_rows = total_rows - n_batch * seq
    cos = jnp.concatenate([jnp.tile(cos, (n_batch, 1)), jnp.ones((n_ctx_rows, HEAD_DIM), F32)], axis=0)
    sin = jnp.concatenate([jnp.tile(sin, (n_batch, 1)), jnp.zeros((n_ctx_rows, HEAD_DIM), F32)], axis=0)
    return cos, sin


def kernel(x, c, ctx, c_ctx, ada_down, ada_up, ada_bias, norm1_g, norm2_g, swa_w_qkv, swa_w_o, swa_q_gain, swa_k_gain, swa_sink, fnet_w_o, nat_w_qkv, nat_w_o, nat_q_gain, nat_k_gain, nat_rpb, moe_router, moe_bias, moe_w_gate, moe_w_up, moe_w_down, shared_w_gate, shared_w_up, shared_w_down):
    n_batch, seq, _ = x.shape
    ctx_len = ctx.shape[1]
    lat_rows = n_batch * seq
    all_rows = lat_rows + n_batch * ctx_len
    assert n_batch + 1 <= COND_ROWS and seq % ROW_TILE == 0 and lat_rows % ROW_TILE == 0 and all_rows % ROW_TILE == 0
    assert seq // GRID_W >= NAT_KH and seq % NAT_BLOCK == 0 and ctx_len == NAT_BLOCK

    h = jnp.concatenate([x.reshape(lat_rows, D_MODEL), ctx.reshape(n_batch * ctx_len, D_MODEL)], axis=0)
    cond = jnp.concatenate([c, c_ctx[None, :], jnp.zeros((COND_ROWS - n_batch - 1, D_MODEL), F32)], axis=0)
    mods_all = _ada_mods(cond, ada_down, ada_up, ada_bias).reshape(DEPTH, COND_ROWS, N_MOD, 1, D_MODEL)
    cos, sin = _rope_tables(n_batch, seq, all_rows)
    ones_row = jnp.ones((N_HEADS * HEAD_DIM,), F32)

    slot = [0, 0, 0]
    for layer in range(DEPTH):
        kind = layer % 3
        s = slot[kind]
        slot[kind] += 1
        ctx_out = layer < DEPTH - 1
        rows = all_rows if ctx_out else lat_rows
        mods = mods_all[layer]
        common = dict(n_batch=n_batch, seq=seq, ctx_len=ctx_len, out_rows=rows)
        xn = _norm_mod(h, norm1_g[layer], mods, 0, 1, all_rows, seq, n_batch)
        if kind == 0:
            nq, nkv = N_HEADS * HEAD_DIM, N_KV_HEADS * HEAD_DIM
            gain = jnp.concatenate([jnp.tile(swa_q_gain[s], N_HEADS), jnp.tile(swa_k_gain[s], N_KV_HEADS),
                                    ones_row[:nkv]])[None, :]
            qkv = _qkv_proj(xn, swa_w_qkv, s, gain, cos, sin, nq + nkv, rope=True)
            att = dict(kind="swa", k_col0=N_HEADS, v_col0=N_HEADS + N_KV_HEADS, group=GQA_GROUP,
                       sink=swa_sink[s], **common)
            o = _attention(qkv, latent=True, **att)
            if ctx_out:
                o = _attention(qkv, latent=False, prev=o, **att)
            w_o = swa_w_o
        elif kind == 1:
            o = _fourier_mix(xn, all_rows, n_batch, seq, ctx_len, ctx_out)
            w_o = fnet_w_o
        else:
            width = N_HEADS * HEAD_DIM
            gain = jnp.concatenate([jnp.tile(nat_q_gain[s], N_HEADS), jnp.tile(nat_k_gain[s], N_HEADS),
                                    ones_row])[None, :]
            qkv = _qkv_proj(xn, nat_w_qkv, s, gain, cos, sin, 2 * width, rope=False)
            att = dict(kind="nat", k_col0=N_HEADS, v_col0=2 * N_HEADS, group=1, **common)
            o = _attention(qkv, latent=True, bias=_nat_bias_table(nat_rpb[s]), **att)
            if ctx_out:
                o = _attention(qkv, latent=False, prev=o, **att)
            w_o = nat_w_o
        h = _mm_residual(o, w_o, s, h, mods, 2, rows, seq, n_batch)
        h = _moe(h, norm2_g[layer], mods, layer, rows, seq, n_batch, moe_router[layer], moe_bias[layer],
                 moe_w_gate, moe_w_up, moe_w_down, shared_w_gate, shared_w_up, shared_w_down)
    return h[:lat_rows].reshape(n_batch, seq, D_MODEL)
```

```python
import functools

import numpy as np
import jax
import jax.numpy as jnp
from jax import lax
from jax.experimental import pallas as pl
from jax.experimental.pallas import tpu as pltpu

F32 = jnp.float32
BF16 = jnp.bfloat16
U32 = jnp.uint32
I32 = jnp.int32

D_MODEL = 4096
DEPTH = 4
GRID_W = 64
HEAD_DIM = 128
N_HEADS = 32
N_KV_HEADS = 8
GQA_GROUP = N_HEADS // N_KV_HEADS
WINDOW = 128
ROPE_THETA = 10000.0
ROPE_AXIS_DIM = HEAD_DIM // 2
NAT_KH = 8
NAT_KW = 16
NAT_ROWS_PER_BLOCK = 4
NAT_BLOCK = NAT_ROWS_PER_BLOCK * GRID_W
FOURIER_GROUPS = 8
FOURIER_GROUP_DIM = D_MODEL // FOURIER_GROUPS
N_EXPERTS = 32
TOP_K = 4
N_EXPERT_GROUPS = 4
GROUP_SIZE = N_EXPERTS // N_EXPERT_GROUPS
TOPK_GROUPS = 2
EXPERT_HIDDEN = 256
SHARED_HIDDEN = 1024
ROUTED_SCALE = 2.5
ADA_RANK = 256
N_MOD = 6
EPS = 1e-6
MASKED = -1e30

VMEM_LIMIT_BYTES = 56 * 1024 * 1024
ROW_TILE = 1024
COL_TILE = 512
EXPERT_TILE = 256
TOKEN_TILE = 256
HALF_D = D_MODEL // 2
LANES = 128
SLAB = HALF_D // LANES
COND_ROWS = 16
DMA_UNROLL = 8
N_DISPATCH_BUFS = 3
ATTN_HEADS_PER_STEP = 8


def _params(*sem, **kw):
    return pltpu.CompilerParams(dimension_semantics=sem, vmem_limit_bytes=VMEM_LIMIT_BYTES, **kw)


def _dot(a, b):
    return jnp.dot(a, b, preferred_element_type=F32)


def _dot_nt(a, b):
    return lax.dot_general(a, b, (((1,), (1,)), ((), ())), preferred_element_type=F32)


def _silu(x):
    return x * (1.0 / (1.0 + jnp.exp(-x)))


def _pack_halves(lo, hi):
    lo_bits = lax.bitcast_convert_type(lo.astype(BF16).astype(F32), U32)
    hi_bits = lax.bitcast_convert_type(hi.astype(BF16).astype(F32), U32)
    return (lo_bits >> 16) | hi_bits


def _unpack_halves(word):
    lo = lax.bitcast_convert_type(word << 16, F32)
    hi = lax.bitcast_convert_type(word & jnp.uint32(0xFFFF0000), F32)
    return lo, hi


def _slab_rows(first_token, n_tokens, part):
    return pl.ds(first_token * SLAB + part, n_tokens, stride=SLAB)


def _store_slabs(ref, packed, first_token=0):
    n = packed.shape[0]
    for part in range(SLAB):
        ref[_slab_rows(first_token, n, part), :] = packed[:, part * LANES:(part + 1) * LANES]


def _load_slabs(ref, n, first_token=0):
    return jnp.concatenate([ref[_slab_rows(first_token, n, part), :] for part in range(SLAB)], axis=1)


def _ada_kernel(cond_ref, down_ref, up_ref, bias_ref, o_ref):
    c = cond_ref[...]
    z = _dot(_silu(c).astype(BF16), down_ref[0].astype(BF16))
    o_ref[0] = _dot(z.astype(BF16), up_ref[0].astype(BF16)) + bias_ref[0]


def _ada_mods(cond, ada_down, ada_up, ada_bias):
    return pl.pallas_call(
        _ada_kernel,
        grid=(DEPTH, N_MOD),
        in_specs=[
            pl.BlockSpec((COND_ROWS, D_MODEL), lambda l, j: (0, 0)),
            pl.BlockSpec((1, D_MODEL, ADA_RANK), lambda l, j: (l, 0, 0)),
            pl.BlockSpec((1, ADA_RANK, D_MODEL), lambda l, j: (l, 0, j)),
            pl.BlockSpec((1, 1, D_MODEL), lambda l, j: (l, 0, j)),
        ],
        out_specs=pl.BlockSpec((1, COND_ROWS, D_MODEL), lambda l, j: (l, 0, j)),
        out_shape=jax.ShapeDtypeStruct((DEPTH, COND_ROWS, N_MOD * D_MODEL), F32),
        compiler_params=_params("arbitrary", "arbitrary"),
        name="ada_mods",
    )(cond, ada_down, ada_up, ada_bias[:, None, :])


def _norm_mod_tile(h_ref, g_ref, shift_ref, scale_ref):
    x = h_ref[...]
    ms = jnp.mean(x * x, axis=-1, keepdims=True)
    y = x * lax.rsqrt(ms + EPS) * g_ref[...]
    return y * (1.0 + scale_ref[0, 0]) + shift_ref[0, 0]


def _norm_mod_kernel(h_ref, g_ref, shift_ref, scale_ref, o_ref):
    o_ref[...] = _norm_mod_tile(h_ref, g_ref, shift_ref, scale_ref).astype(BF16)


def _split_bf16(x):
    hi = x.astype(BF16)
    lo = (x - hi.astype(F32)).astype(BF16)
    return hi, lo


def _first_index_of_max(v, idx, n):
    m = jnp.max(v, axis=0, keepdims=True)
    i = jnp.min(jnp.where(v == m, idx, n), axis=0, keepdims=True)
    return m, i


def _norm_route_kernel(h_ref, g_ref, shift_ref, scale_ref, wr_ref, eb_ref,
                       o_ref, xw_ref, idx_ref, wts_ref, rank_ref, cnt_ref, run_ref):
    @pl.when(pl.program_id(0) == 0)
    def _():
        run_ref[...] = jnp.zeros_like(run_ref)

    xn = _norm_mod_tile(h_ref, g_ref, shift_ref, scale_ref)
    x_hi = xn.astype(BF16)
    hi_f32 = x_hi.astype(F32)
    o_ref[...] = x_hi
    hi_bits = lax.bitcast_convert_type(hi_f32, U32)
    _store_slabs(xw_ref, (hi_bits[:, :HALF_D] >> 16) | hi_bits[:, HALF_D:])
    x_lo = (xn - hi_f32).astype(BF16)
    prod_t = (_dot(x_hi, wr_ref[...]) + _dot(x_lo, wr_ref[...])).T
    lt = prod_t[:N_EXPERTS] + prod_t[N_EXPERTS:2 * N_EXPERTS]
    scores = 1.0 / (1.0 + jnp.exp(-lt))
    sel = scores + eb_ref[...]
    n_tok = sel.shape[1]
    sub = lax.broadcasted_iota(I32, (GROUP_SIZE, n_tok), 0).astype(F32)
    grp_scores = []
    for g in range(N_EXPERT_GROUPS):
        sg = sel[g * GROUP_SIZE:(g + 1) * GROUP_SIZE]
        m1, i1 = _first_index_of_max(sg, sub, float(GROUP_SIZE))
        m2 = jnp.max(jnp.where(sub == i1, -jnp.inf, sg), axis=0, keepdims=True)
        grp_scores.append(m1 + m2)
    eidx_i = lax.broadcasted_iota(I32, sel.shape, 0)
    eidx = eidx_i.astype(F32)
    egrp = (eidx_i // GROUP_SIZE).astype(F32)
    best = jnp.full((1, n_tok), -jnp.inf, F32)
    for g in range(N_EXPERT_GROUPS):
        best = jnp.maximum(best, grp_scores[g])
    g1 = jnp.full((1, n_tok), float(N_EXPERT_GROUPS), F32)
    for g in reversed(range(N_EXPERT_GROUPS)):
        g1 = jnp.where(grp_scores[g] == best, float(g), g1)
    second = jnp.full((1, n_tok), -jnp.inf, F32)
    for g in range(N_EXPERT_GROUPS):
        second = jnp.maximum(second, jnp.where(g1 == float(g), -jnp.inf, grp_scores[g]))
    g2 = jnp.full((1, n_tok), float(N_EXPERT_GROUPS), F32)
    for g in reversed(range(N_EXPERT_GROUPS)):
        g2 = jnp.where((grp_scores[g] == second) & (g1 != float(g)), float(g), g2)
    masked = jnp.where((egrp == g1) | (egrp == g2), sel, -jnp.inf)
    picks, weights, hits = [], [], []
    for _ in range(TOP_K):
        _, ik = _first_index_of_max(masked, eidx, float(N_EXPERTS))
        hit = eidx == ik
        weights.append(jnp.sum(jnp.where(hit, scores, 0.0), axis=0, keepdims=True))
        picks.append(ik)
        hits.append(hit)
        masked = jnp.where(hit, -jnp.inf, masked)
    w = jnp.concatenate(weights, axis=0)
    idx_ref[...] = jnp.concatenate(picks, axis=0).astype(I32)
    wts_ref[...] = w / jnp.sum(w, axis=0, keepdims=True) * ROUTED_SCALE
    member = jnp.zeros(sel.shape, F32)
    for hit in hits:
        member = jnp.where(hit, 1.0, member)
    earlier = lax.broadcasted_iota(I32, (n_tok, n_tok), 0) < lax.broadcasted_iota(I32, (n_tok, n_tok), 1)
    before = run_ref[...] + _dot(member.astype(BF16), jnp.where(earlier, 1.0, 0.0).astype(BF16))
    rank_ref[...] = jnp.concatenate(
        [jnp.sum(jnp.where(hit, before, 0.0), axis=0, keepdims=True) for hit in hits], axis=0).astype(I32)
    run_ref[...] = run_ref[...] + jnp.sum(member, axis=1, keepdims=True)
    cnt_ref[...] = run_ref[...]


def _seg_of_tile(i, tm, seq, n_batch):
    return jnp.minimum((i * tm) // seq, n_batch)


def _norm_mod(h, g, mods, shift_k, scale_k, rows, seq, n_batch, router=None):
    tm = TOKEN_TILE
    seg = functools.partial(_seg_of_tile, tm=tm, seq=seq, n_batch=n_batch)
    in_specs = [
        pl.BlockSpec((tm, D_MODEL), lambda i: (i, 0)),
        pl.BlockSpec((1, D_MODEL), lambda i: (0, 0)),
        pl.BlockSpec((1, 1, 1, D_MODEL), lambda i: (seg(i), shift_k, 0, 0)),
        pl.BlockSpec((1, 1, 1, D_MODEL), lambda i: (seg(i), scale_k, 0, 0)),
    ]
    xn_spec = pl.BlockSpec((tm, D_MODEL), lambda i: (i, 0))
    xn_shape = jax.ShapeDtypeStruct((rows, D_MODEL), BF16)
    if router is None:
        return pl.pallas_call(
            _norm_mod_kernel, grid=(rows // tm,), in_specs=in_specs, out_specs=xn_spec, out_shape=xn_shape,
            compiler_params=_params("arbitrary"), name="norm_mod",
        )(h, g.reshape(1, D_MODEL), mods, mods)
    w_router, e_bias = router
    w_hi, w_lo = _split_bf16(w_router)
    w_pad = jnp.pad(jnp.concatenate([w_hi, w_lo], axis=1), ((0, 0), (0, LANES - 2 * N_EXPERTS)))
    in_specs += [
        pl.BlockSpec((D_MODEL, 128), lambda i: (0, 0)),
        pl.BlockSpec((N_EXPERTS, 1), lambda i: (0, 0)),
    ]
    per_tok = pl.BlockSpec((TOP_K, tm), lambda i: (0, i))
    return pl.pallas_call(
        _norm_route_kernel, grid=(rows // tm,), in_specs=in_specs,
        out_specs=[xn_spec, pl.BlockSpec((tm * SLAB, LANES), lambda i: (i, 0)), per_tok, per_tok, per_tok,
                   pl.BlockSpec((N_EXPERTS, 1), lambda i: (0, 0))],
        out_shape=[xn_shape, jax.ShapeDtypeStruct((rows * SLAB, LANES), U32),
                   jax.ShapeDtypeStruct((TOP_K, rows), I32), jax.ShapeDtypeStruct((TOP_K, rows), F32),
                   jax.ShapeDtypeStruct((TOP_K, rows), I32), jax.ShapeDtypeStruct((N_EXPERTS, 1), F32)],
        scratch_shapes=[pltpu.VMEM((N_EXPERTS, 1), F32)],
        compiler_params=_params("arbitrary"), name="norm_route",
    )(h, g.reshape(1, D_MODEL), mods, mods, w_pad, e_bias.reshape(N_EXPERTS, 1))


def _cast_weight_once(w_ref, w_scr):
    @pl.when(pl.program_id(1) == 0)
    def _():
        w_scr[...] = w_ref[0].astype(BF16)


def _qkv_kernel(a_ref, w_ref, gain_ref, cos_ref, sin_ref, o_ref, w_scr, *, n_qk_tiles, rope):
    _cast_weight_once(w_ref, w_scr)
    j = pl.program_id(0)
    tm, tn = o_ref.shape
    part = tn if rope else 2 * HEAD_DIM
    whole = _dot(a_ref[...], w_scr[...]) if rope else None

    @pl.when(j < n_qk_tiles)
    def _():
        lane = lax.broadcasted_iota(I32, (tm, HEAD_DIM), 1)
        quarter = ROPE_AXIS_DIM // 2
        first_half = (lane % ROPE_AXIS_DIM) < quarter
        for p0 in range(0, tn, part):
            acc = whole if rope else _dot(a_ref[...], w_scr[:, p0:p0 + part])
            outs = []
            for h0 in range(0, part, HEAD_DIM):
                x = acc[:, h0:h0 + HEAD_DIM]
                ms = jnp.mean(x * x, axis=-1, keepdims=True)
                y = x * lax.rsqrt(ms + EPS) * gain_ref[:, p0 + h0:p0 + h0 + HEAD_DIM]
                if rope:
                    partner = jnp.where(first_half, pltpu.roll(y, HEAD_DIM - quarter, 1), pltpu.roll(y, quarter, 1))
                    y = y * cos_ref[...] + partner * sin_ref[...]
                outs.append(y.astype(BF16))
            o_ref[:, p0:p0 + part] = jnp.concatenate(outs, axis=1)

    @pl.when(j >= n_qk_tiles)
    def _():
        o_ref[...] = (whole if rope else _dot(a_ref[...], w_scr[...])).astype(BF16)


def _qkv_proj(xn, w_stack, w_slot, gain_row, cos, sin, n_qk_cols, rope):
    rows, width = xn.shape[0], w_stack.shape[2]
    tm, tn = ROW_TILE, COL_TILE
    return pl.pallas_call(
        functools.partial(_qkv_kernel, n_qk_tiles=n_qk_cols // tn, rope=rope),
        grid=(width // tn, rows // tm),
        in_specs=[
            pl.BlockSpec((tm, D_MODEL), lambda j, i: (i, 0)),
            pl.BlockSpec((1, D_MODEL, tn), lambda j, i: (w_slot, 0, j)),
            pl.BlockSpec((1, tn), lambda j, i: (0, j)),
            pl.BlockSpec((tm, HEAD_DIM), lambda j, i: (i, 0)),
            pl.BlockSpec((tm, HEAD_DIM), lambda j, i: (i, 0)),
        ],
        out_specs=pl.BlockSpec((tm, tn), lambda j, i: (i, j)),
        out_shape=jax.ShapeDtypeStruct((rows, width), BF16),
        scratch_shapes=[pltpu.VMEM((D_MODEL, tn), BF16)],
        compiler_params=_params("arbitrary", "arbitrary"),
        name="qkv_proj",
    )(xn, w_stack, gain_row, cos, sin)


def _mm_res_kernel(a_ref, w_ref, h_ref, gate_ref, o_ref, w_scr):
    _cast_weight_once(w_ref, w_scr)
    o_ref[...] = h_ref[...] + gate_ref[0, 0] * _dot(a_ref[...], w_scr[...])


def _mm_residual(a, w_stack, w_slot, h, mods, gate_k, rows, seq, n_batch):
    _, k_dim, width = w_stack.shape
    tm, tn = ROW_TILE, COL_TILE
    seg = functools.partial(_seg_of_tile, tm=tm, seq=seq, n_batch=n_batch)
    return pl.pallas_call(
        _mm_res_kernel,
        grid=(width // tn, rows // tm),
        in_specs=[
            pl.BlockSpec((tm, k_dim), lambda j, i: (i, 0)),
            pl.BlockSpec((1, k_dim, tn), lambda j, i: (w_slot, 0, j)),
            pl.BlockSpec((tm, tn), lambda j, i: (i, j)),
            pl.BlockSpec((1, 1, 1, tn), lambda j, i: (seg(i), gate_k, 0, j)),
        ],
        out_specs=pl.BlockSpec((tm, tn), lambda j, i: (i, j)),
        out_shape=jax.ShapeDtypeStruct((rows, width), F32),
        scratch_shapes=[pltpu.VMEM((k_dim, tn), BF16)],
        compiler_params=_params("arbitrary", "arbitrary"),
        name="mm_residual",
    )(a, w_stack, h, mods)


def _attn_kernel(*refs, n_local, group, block, kind, has_sink, has_prev, n_blocks, heads):
    refs = list(refs)
    sink_ref = refs.pop(0) if has_sink else None
    q_ref = refs.pop(0)
    k_loc = [refs.pop(0) for _ in range(n_local)]
    v_loc = [refs.pop(0) for _ in range(n_local)]
    kc_ref, vc_ref = refs.pop(0), refs.pop(0)
    bias_ref = refs.pop(0) if kind == "nat" else None
    if has_prev:
        refs.pop(0)
    o_ref = refs.pop(0)
    head0 = pl.program_id(1) * heads
    i = pl.program_id(2)
    scale = HEAD_DIM ** -0.5
    n_rows = group * block

    valid = None
    if n_local:
        row = lax.broadcasted_iota(I32, (n_rows, 1), 0)
        col = lax.broadcasted_iota(I32, (1, n_local * block), 1)
        if kind == "swa":
            rel = col - block - (row % block)
            lo = jnp.where(i == 0, block, 0)
            hi = jnp.where(i == n_blocks - 1, 2 * block, 3 * block)
            valid = (rel >= -WINDOW) & (rel <= WINDOW) & (col >= lo) & (col < hi)
        else:
            q_row = NAT_ROWS_PER_BLOCK * i + row // GRID_W
            k_row = NAT_ROWS_PER_BLOCK * (i - 1) + col // GRID_W
            r0 = jnp.clip(q_row - NAT_KH // 2, 0, n_blocks * NAT_ROWS_PER_BLOCK - NAT_KH)
            valid = (k_row >= r0) & (k_row < r0 + NAT_KH)
    g_of_row = lax.broadcasted_iota(I32, (n_rows, 1), 0) // block if has_sink else None

    outs = []
    for hd in range(heads):
        kv_cols = slice(hd * HEAD_DIM, (hd + 1) * HEAD_DIM)
        if group > 1:
            q = jnp.concatenate([q_ref[:, (hd * group + g) * HEAD_DIM:(hd * group + g + 1) * HEAD_DIM]
                                 for g in range(group)], axis=0)
        else:
            q = q_ref[:, kv_cols]
        s_ctx = _dot_nt(q, kc_ref[:, kv_cols]) * scale
        m = jnp.max(s_ctx, axis=-1, keepdims=True)
        if n_local:
            k = jnp.concatenate([r[:, kv_cols] for r in k_loc], axis=0)
            v = jnp.concatenate([r[:, kv_cols] for r in v_loc], axis=0)
            s_loc = _dot_nt(q, k) * scale
            if kind == "nat":
                s_loc = s_loc + bias_ref[hd]
            s_loc = jnp.where(valid, s_loc, MASKED)
            m = jnp.maximum(m, jnp.max(s_loc, axis=-1, keepdims=True))
        if has_sink:
            sink = jnp.zeros((n_rows, 1), F32)
            for g in range(group):
                sink = jnp.where(g_of_row == g, sink_ref[(head0 + hd) * group + g], sink)
            m = jnp.maximum(m, sink)
        p_ctx = jnp.exp(s_ctx - m)
        denom = jnp.sum(p_ctx, axis=-1, keepdims=True)
        o = _dot(p_ctx.astype(BF16), vc_ref[:, kv_cols])
        if n_local:
            p_loc = jnp.exp(s_loc - m)
            denom = denom + jnp.sum(p_loc, axis=-1, keepdims=True)
            o = o + _dot(p_loc.astype(BF16), v)
        if has_sink:
            denom = denom + jnp.exp(sink - m)
        o = (o * (1.0 / denom)).astype(BF16)
        outs += [o[g * block:(g + 1) * block] for g in range(group)]
    o_ref[...] = jnp.concatenate(outs, axis=1) if len(outs) > 1 else outs[0]


def _attention(qkv, *, kind, latent, n_batch, seq, ctx_len, k_col0, v_col0, group, out_rows,
               sink=None, bias=None, prev=None):
    heads = ATTN_HEADS_PER_STEP
    n_kv = v_col0 - k_col0
    assert n_kv % heads == 0 and k_col0 % heads == 0 and v_col0 % heads == 0
    n_steps_h = n_kv // heads
    kc0, vc0 = k_col0 // heads, v_col0 // heads
    block = (WINDOW if kind == "swa" else NAT_BLOCK) if latent else ctx_len
    n_blocks = seq // block if latent else 1
    n_local = 3 if latent else 0
    ctx_blk0 = n_batch * seq // ctx_len
    q_blk0 = 0 if latent else n_batch * seq // block
    has_sink = sink is not None
    nat_order = kind == "nat"

    def ids(a, b, c):
        return (b, a, c) if nat_order else (a, b, c)

    def q_map(a, b, c):
        bb, hh, ii = ids(a, b, c)
        return (q_blk0 + bb * n_blocks + ii, hh)

    def loc_map(delta, col0):
        def f(a, b, c):
            bb, hh, ii = ids(a, b, c)
            return (bb * n_blocks + jnp.clip(ii + delta, 0, n_blocks - 1), col0 + hh)
        return f

    def ctx_map(col0):
        def f(a, b, c):
            bb, hh, _ = ids(a, b, c)
            return (ctx_blk0 + bb, col0 + hh)
        return f

    in_specs, args = [], []
    if has_sink:
        in_specs.append(pl.BlockSpec(memory_space=pltpu.SMEM))
        args.append(sink.astype(F32))
    in_specs.append(pl.BlockSpec((block, heads * group * HEAD_DIM), q_map))
    args.append(qkv)
    for col0 in ((kc0, vc0) if latent else ()):
        for delta in (-1, 0, 1):
            in_specs.append(pl.BlockSpec((block, heads * HEAD_DIM), loc_map(delta, col0)))
            args.append(qkv)
    for col0 in (kc0, vc0):
        in_specs.append(pl.BlockSpec((ctx_len, heads * HEAD_DIM), ctx_map(col0)))
        args.append(qkv)
    use_bias = nat_order and latent
    if use_bias:
        in_specs.append(pl.BlockSpec((heads, block, 3 * block), lambda a, b, c: (a, 0, 0)))
        args.append(bias)
    aliases = {}
    if prev is not None:
        aliases = {len(args): 0}
        in_specs.append(pl.BlockSpec(memory_space=pl.ANY))
        args.append(prev)
    grid = (n_steps_h, n_batch, n_blocks) if nat_order else (n_batch, n_steps_h, n_blocks)
    kern = functools.partial(_attn_kernel, n_local=n_local, group=group, block=block,
                             kind=("nat" if use_bias else kind if latent else "ctx"),
                             has_sink=has_sink, has_prev=prev is not None, n_blocks=n_blocks, heads=heads)
    return pl.pallas_call(
        kern, grid=grid, in_specs=in_specs,
        out_specs=pl.BlockSpec((block, heads * group * HEAD_DIM), q_map),
        out_shape=jax.ShapeDtypeStruct((out_rows, n_kv * group * HEAD_DIM), BF16),
        input_output_aliases=aliases,
        compiler_params=_params("arbitrary", "arbitrary", "arbitrary"),
        name=f"attn_{kind}_{'latent' if latent else 'ctx'}",
    )(*args)


def _nat_bias_table(rpb):
    q_c = np.arange(GRID_W)[:, None]
    k_c = np.arange(GRID_W)[None, :]
    d_col = np.clip(k_c - q_c + NAT_KW - 1, 0, 2 * NAT_KW - 2)
    c0 = np.clip(q_c - NAT_KW // 2, 0, GRID_W - NAT_KW)
    col_valid = (k_c >= c0) & (k_c < c0 + NAT_KW)
    onehot = (np.arange(2 * NAT_KW - 1)[:, None, None] == d_col[None]).astype(np.float32)
    by_col = jnp.einsum("hrd,dqk->hrqk", rpb.astype(F32), jnp.asarray(onehot), precision=lax.Precision.HIGHEST)
    by_col = jnp.where(col_valid[None, None], by_col, MASKED)
    n_key_rows = 3 * NAT_ROWS_PER_BLOCK
    rows = []
    for q_r in range(NAT_ROWS_PER_BLOCK):
        d_row = [k_r - NAT_ROWS_PER_BLOCK - q_r + NAT_KH - 1 for k_r in range(n_key_rows)]
        rows.append(jnp.concatenate([by_col[:, d] for d in d_row], axis=-1))
    return jnp.concatenate(rows, axis=1)


def _dft_chan_kernel(a_ref, w_ref, o_ref):
    o_ref[0] = _dot(a_ref[...], w_ref[0]).astype(BF16)


def _dft_pos_kernel(cn_ref, sn_ref, xc_ref, xs_ref, *rest, scale):
    o_ref = rest[-1]
    o_ref[...] = ((_dot(cn_ref[...], xc_ref[0]) - _dot(sn_ref[...], xs_ref[0])) * scale).astype(BF16)


def _dft_tables(n):
    k = jnp.arange(n, dtype=I32)
    ang = ((k[:, None] * k[None, :]) % n).astype(F32) * (2.0 * np.pi / n)
    return jnp.cos(ang).astype(BF16), jnp.sin(ang).astype(BF16)


def _fourier_mix(xn, rows, n_batch, seq, ctx_len, ctx_out):
    gd = FOURIER_GROUP_DIM
    cd, sd = _dft_tables(gd)
    chan = jnp.stack([cd, sd])
    tm = ROW_TILE
    xcs = pl.pallas_call(
        _dft_chan_kernel,
        grid=(rows // tm, FOURIER_GROUPS, 2),
        in_specs=[pl.BlockSpec((tm, gd), lambda i, g, t: (i, g)),
                  pl.BlockSpec((1, gd, gd), lambda i, g, t: (t, 0, 0))],
        out_specs=pl.BlockSpec((1, tm, gd), lambda i, g, t: (t, i, g)),
        out_shape=jax.ShapeDtypeStruct((2, rows, D_MODEL), BF16),
        compiler_params=_params("arbitrary", "arbitrary", "arbitrary"),
        name="dft_channels",
    )(xn, chan)

    def pos_call(n, tmo, row_blk0, in_blk0, out_rows, prev):
        cn, sn = _dft_tables(n)
        tn = COL_TILE
        per = n // tmo
        in_specs = [
            pl.BlockSpec((tmo, n), lambda b, i, j: (i, 0)),
            pl.BlockSpec((tmo, n), lambda b, i, j: (i, 0)),
            pl.BlockSpec((1, n, tn), lambda b, i, j: (0, in_blk0 + b, j)),
            pl.BlockSpec((1, n, tn), lambda b, i, j: (1, in_blk0 + b, j)),
        ]
        args = [cn, sn, xcs, xcs]
        aliases = {}
        if prev is not None:
            in_specs.append(pl.BlockSpec(memory_space=pl.ANY))
            args.append(prev)
            aliases = {4: 0}
        return pl.pallas_call(
            functools.partial(_dft_pos_kernel, scale=float((n * gd) ** -0.5)),
            grid=(n_batch, per, D_MODEL // tn),
            in_specs=in_specs,
            out_specs=pl.BlockSpec((tmo, tn), lambda b, i, j: (row_blk0 + b * per + i, j)),
            out_shape=jax.ShapeDtypeStruct((out_rows, D_MODEL), BF16),
            input_output_aliases=aliases,
            compiler_params=_params("arbitrary", "arbitrary", "arbitrary"),
            name=f"dft_positions_{n}",
        )(*args)

    z = pos_call(seq, 512, 0, 0, rows, None)
    if ctx_out:
        z = pos_call(ctx_len, ctx_len, n_batch * seq // ctx_len, n_batch * seq // ctx_len, rows, z)
    return z


def _slab_copy(src_ref, src_token, dst_ref, dst_token, sem):
    src = src_ref.at[pl.ds(pl.multiple_of(src_token * SLAB, SLAB), SLAB)]
    dst = dst_ref.at[pl.ds(pl.multiple_of(dst_token * SLAB, SLAB), SLAB)]
    return pltpu.make_async_copy(src, dst, sem)


def _load_slots(pos_ref, pos_smem, sem):
    cp = pltpu.make_async_copy(pos_ref.at[0, 0], pos_smem, sem)
    cp.start()
    cp.wait()


def _for_token_groups(n_tokens, fn):
    def group(g, c):
        for k in range(TOP_K):
            for u in range(DMA_UNROLL):
                fn(k, g * DMA_UNROLL + u)
        return c
    lax.fori_loop(0, n_tokens // DMA_UNROLL, group, 0)


def _dispatch_step(i, n_steps, cnt_ref, off_ref, pos_ref, xw_ref, xs_ref,
                   pos_smem, zero_ref, xbuf_ref, sem_idx, sem_load, sem_rows):
    tm = pos_smem.shape[0] // TOP_K

    @pl.when(i == 0)
    def _():
        zero_ref[...] = jnp.zeros_like(zero_ref)

        def per_expert(e, carry):
            count = cnt_ref[e]
            first = off_ref[e] + count
            n_pad = (count + EXPERT_TILE - 1) // EXPERT_TILE * EXPERT_TILE - count

            def start(r, c):
                _slab_copy(zero_ref, 0, xs_ref, first + r, sem_rows.at[0]).start()
                return c

            def wait(r, c):
                _slab_copy(zero_ref, 0, xs_ref, first + r, sem_rows.at[0]).wait()
                return c

            lax.fori_loop(0, n_pad, start, 0)
            lax.fori_loop(0, n_pad, wait, 0)
            return carry

        lax.fori_loop(0, N_EXPERTS, per_expert, 0)

    def load(step):
        which = step % N_DISPATCH_BUFS
        src = xw_ref.at[pl.ds(pl.multiple_of(step * tm * SLAB, tm * SLAB), tm * SLAB)]
        return pltpu.make_async_copy(src, xbuf_ref.at[which], sem_load.at[which])

    def drain(step):
        which = step % N_DISPATCH_BUFS
        _for_token_groups(tm, lambda k, t: _slab_copy(xbuf_ref.at[which], t, xs_ref, 0, sem_rows.at[which]).wait())

    @pl.when(i == 0)
    def _():
        load(i).start()

    @pl.when(i >= N_DISPATCH_BUFS - 1)
    def _():
        drain(i - (N_DISPATCH_BUFS - 1))

    @pl.when(i + 1 < n_steps)
    def _():
        load(i + 1).start()

    load(i).wait()
    _load_slots(pos_ref, pos_smem, sem_idx)
    cur = i % N_DISPATCH_BUFS
    _for_token_groups(tm, lambda k, t: _slab_copy(
        xbuf_ref.at[cur], t, xs_ref, pos_smem[k * tm + t], sem_rows.at[cur]).start())

    @pl.when(i == n_steps - 1)
    def _():
        for back in reversed(range(N_DISPATCH_BUFS - 1)):
            @pl.when(i - back >= 0)
            def _(back=back):
                drain(i - back)


def _swiglu_dispatch_kernel(cnt_ref, off_ref, a_ref, wg_ref, wu_ref, pos_ref, xw_ref, o_ref, xs_ref,
                            wg_scr, wu_scr, *dispatch_scratch):
    n_inner = pl.num_programs(1)
    step = pl.program_id(0) * n_inner + pl.program_id(1)
    _dispatch_step(step, pl.num_programs(0) * n_inner, cnt_ref, off_ref, pos_ref, xw_ref, xs_ref, *dispatch_scratch)
    _cast_weight_once(wg_ref, wg_scr)
    _cast_weight_once(wu_ref, wu_scr)
    a = a_ref[...]
    o_ref[...] = (_silu(_dot(a, wg_scr[...])) * _dot(a, wu_scr[...])).astype(BF16)


def _swiglu_and_dispatch(a, wg_stack, wu_stack, layer, rows, xw, pos_tiles, counts, offs, n_slots):
    width = wg_stack.shape[2]
    tm, tn, tok = ROW_TILE, COL_TILE // 2, TOKEN_TILE
    n_inner = rows // tm
    assert (width // tn) * n_inner == rows // tok
    w_spec = pl.BlockSpec((1, D_MODEL, tn), lambda j, i, cnt, off: (layer, 0, j))
    grid_spec = pltpu.PrefetchScalarGridSpec(
        num_scalar_prefetch=2,
        grid=(width // tn, n_inner),
        in_specs=[
            pl.BlockSpec((tm, D_MODEL), lambda j, i, cnt, off: (i, 0)),
            w_spec, w_spec,
            pl.BlockSpec((1, 1, TOP_K * tok), lambda j, i, cnt, off: (j * n_inner + i, 0, 0)),
            pl.BlockSpec(memory_space=pl.ANY),
        ],
        out_specs=[pl.BlockSpec((tm, tn), lambda j, i, cnt, off: (i, j)), pl.BlockSpec(memory_space=pl.ANY)],
        scratch_shapes=[
            pltpu.VMEM((D_MODEL, tn), BF16),
            pltpu.VMEM((D_MODEL, tn), BF16),
            pltpu.SMEM((TOP_K * tok,), I32),
            pltpu.VMEM((SLAB, LANES), U32),
            pltpu.VMEM((N_DISPATCH_BUFS, tok * SLAB, LANES), U32),
            pltpu.SemaphoreType.DMA,
            pltpu.SemaphoreType.DMA((N_DISPATCH_BUFS,)),
            pltpu.SemaphoreType.DMA((N_DISPATCH_BUFS,)),
        ],
    )
    return pl.pallas_call(
        _swiglu_dispatch_kernel, grid_spec=grid_spec,
        out_shape=[jax.ShapeDtypeStruct((rows, width), BF16), jax.ShapeDtypeStruct((n_slots * SLAB, LANES), U32)],
        compiler_params=_params("arbitrary", "arbitrary", disable_bounds_checks=True),
        name="shared_swiglu_dispatch",
    )(counts, offs, a, wg_stack, wu_stack, pos_tiles, xw)


def _expert_kernel(te_ref, nu_ref, x_ref, wg_ref, wu_ref, wd_ref, o_ref, wgu_scr, wd_scr):
    j = pl.program_id(0)

    @pl.when(j < nu_ref[0])
    def _():
        @pl.when((j == 0) | (te_ref[j] != te_ref[jnp.maximum(j - 1, 0)]))
        def _():
            wgu_scr[:, :EXPERT_HIDDEN] = wg_ref[0, 0].astype(BF16)
            wgu_scr[:, EXPERT_HIDDEN:] = wu_ref[0, 0].astype(BF16)
            wd_scr[...] = wd_ref[0, 0].astype(BF16)

        x_lo, x_hi = _unpack_halves(_load_slabs(x_ref, EXPERT_TILE))
        x_lo, x_hi = x_lo.astype(BF16), x_hi.astype(BF16)
        gate_up = _dot(x_lo, wgu_scr[:HALF_D]) + _dot(x_hi, wgu_scr[HALF_D:])
        hid = (_silu(gate_up[:, :EXPERT_HIDDEN]) * gate_up[:, EXPERT_HIDDEN:]).astype(BF16)
        _store_slabs(o_ref, _pack_halves(_dot(hid, wd_scr[:, :HALF_D]), _dot(hid, wd_scr[:, HALF_D:])))


def _experts(x_sorted, tile_expert, n_used, wg, wu, wd, layer):
    n_rows = x_sorted.shape[0] // SLAB
    tm = EXPERT_TILE

    def row_map(j, te, nu):
        return (jnp.minimum(j, nu[0] - 1), 0)

    def w_map(j, te, nu):
        return (layer, te[j], 0, 0)

    grid_spec = pltpu.PrefetchScalarGridSpec(
        num_scalar_prefetch=2,
        grid=(n_rows // tm,),
        in_specs=[
            pl.BlockSpec((tm * SLAB, LANES), row_map),
            pl.BlockSpec((1, 1, D_MODEL, EXPERT_HIDDEN), w_map),
            pl.BlockSpec((1, 1, D_MODEL, EXPERT_HIDDEN), w_map),
            pl.BlockSpec((1, 1, EXPERT_HIDDEN, D_MODEL), w_map),
        ],
        out_specs=pl.BlockSpec((tm * SLAB, LANES), row_map),
        scratch_shapes=[
            pltpu.VMEM((D_MODEL, 2 * EXPERT_HIDDEN), BF16),
            pltpu.VMEM((EXPERT_HIDDEN, D_MODEL), BF16),
        ],
    )
    return pl.pallas_call(
        _expert_kernel, grid_spec=grid_spec,
        out_shape=jax.ShapeDtypeStruct((n_rows * SLAB, LANES), U32),
        compiler_params=_params("arbitrary"),
        name="routed_experts",
    )(tile_expert, n_used, x_sorted, wg, wu, wd)


def _combine_kernel(pos_ref, nxt_ref, h_ref, w_ref, gate_ref, y_ref, o_ref, pos_smem, buf_ref, sem_idx, sem_rows):
    i = pl.program_id(0)
    n_steps = pl.num_programs(0)
    tm = h_ref.shape[0]
    cur = i % 2

    def gather(slot_ref, which):
        _load_slots(slot_ref, pos_smem, sem_idx)
        _for_token_groups(tm, lambda k, t: _slab_copy(
            y_ref, pos_smem[k * tm + t], buf_ref.at[which, k], t, sem_rows.at[which]).start())

    @pl.when(i == 0)
    def _():
        gather(pos_ref, 0)

    @pl.when(i + 1 < n_steps)
    def _():
        gather(nxt_ref, 1 - cur)

    _for_token_groups(tm, lambda k, t: _slab_copy(y_ref, 0, buf_ref.at[cur, k], t, sem_rows.at[cur]).wait())
    sub = 8

    def rows_chunk(r, c):
        first = pl.multiple_of(r * sub, sub)
        rs = pl.ds(first, sub)
        w = w_ref[rs, :]
        wk = [jnp.broadcast_to(w[:, k:k + 1], (sub, LANES)) for k in range(TOP_K)]
        for part in range(SLAB):
            terms = []
            for k in range(TOP_K):
                y_lo, y_hi = _unpack_halves(buf_ref[cur, k, _slab_rows(first, sub, part), :])
                terms.append((wk[k] * y_lo, wk[k] * y_hi))
            acc_lo = (terms[0][0] + terms[1][0]) + (terms[2][0] + terms[3][0])
            acc_hi = (terms[0][1] + terms[1][1]) + (terms[2][1] + terms[3][1])
            lo = pl.ds(part * LANES, LANES)
            hi = pl.ds(HALF_D + part * LANES, LANES)
            o_ref[rs, lo] = h_ref[rs, lo] + gate_ref[0, 0, :, lo] * acc_lo
            o_ref[rs, hi] = h_ref[rs, hi] + gate_ref[0, 0, :, hi] * acc_hi
        return c

    lax.fori_loop(0, tm // sub, rows_chunk, 0)


def _combine(h, y_sorted, pos_tiles, wts_t, mods, gate_k, rows, seq, n_batch):
    tm = TOKEN_TILE
    n_steps = rows // tm
    seg = functools.partial(_seg_of_tile, tm=tm, seq=seq, n_batch=n_batch)
    return pl.pallas_call(
        _combine_kernel,
        grid=(n_steps,),
        in_specs=[
            pl.BlockSpec((1, 1, TOP_K * tm), lambda i: (i, 0, 0)),
            pl.BlockSpec((1, 1, TOP_K * tm), lambda i: (jnp.minimum(i + 1, n_steps - 1), 0, 0)),
            pl.BlockSpec((tm, D_MODEL), lambda i: (i, 0)),
            pl.BlockSpec((tm, TOP_K), lambda i: (i, 0)),
            pl.BlockSpec((1, 1, 1, D_MODEL), lambda i: (seg(i), gate_k, 0, 0)),
            pl.BlockSpec(memory_space=pl.ANY),
        ],
        out_specs=pl.BlockSpec((tm, D_MODEL), lambda i: (i, 0)),
        out_shape=jax.ShapeDtypeStruct((rows, D_MODEL), F32),
        scratch_shapes=[
            pltpu.SMEM((TOP_K * tm,), I32),
            pltpu.VMEM((2, TOP_K, tm * SLAB, LANES), U32),
            pltpu.SemaphoreType.DMA,
            pltpu.SemaphoreType.DMA((2,)),
        ],
        compiler_params=_params("arbitrary", disable_bounds_checks=True),
        name="moe_combine",
    )(pos_tiles, pos_tiles, h, wts_t, mods, y_sorted)


def _dispatch_plan(idx, rank, counts_f, rows):
    tm = EXPERT_TILE
    counts = counts_f[:, 0].astype(I32)
    padded = (counts + tm - 1) // tm * tm
    ends = jnp.cumsum(padded)
    offs = ends - padded
    expert = jnp.arange(N_EXPERTS, dtype=I32)[:, None, None]
    pos = rank + jnp.sum(jnp.where(idx[None] == expert, offs[:, None, None], 0), axis=0)
    n_tok_tiles = rows // TOKEN_TILE
    pos_tiles = pos.reshape(TOP_K, n_tok_tiles, TOKEN_TILE).transpose(1, 0, 2).reshape(n_tok_tiles, 1, TOP_K * TOKEN_TILE)
    n_slots = TOP_K * rows + N_EXPERTS * tm
    n_used = ends[-1] // tm
    tile_start = jnp.arange(n_slots // tm, dtype=I32) * tm
    tile_expert = jnp.sum((tile_start[:, None] >= ends[None, :]).astype(I32), axis=1)
    last_expert = jnp.sum((((n_used - 1) * tm) >= ends).astype(I32))
    tile_expert = jnp.where(tile_start < ends[-1], tile_expert, last_expert).astype(I32)
    return pos_tiles, counts, offs.astype(I32), tile_expert, n_used.astype(I32).reshape(1), n_slots


def _moe(h, g2, mods, layer, rows, seq, n_batch, w_router, e_bias, wg, wu, wd, sg, su, sd):
    xn, xw, idx, wts, rank, counts_f = _norm_mod(h, g2, mods, 3, 4, rows, seq, n_batch, router=(w_router, e_bias))
    pos_tiles, counts, offs, tile_expert, n_used, n_slots = _dispatch_plan(idx, rank, counts_f, rows)
    hid_s, x_sorted = _swiglu_and_dispatch(xn, sg, su, layer, rows, xw, pos_tiles, counts, offs, n_slots)
    h = _mm_residual(hid_s, sd, layer, h, mods, 5, rows, seq, n_batch)
    y_sorted = _experts(x_sorted, tile_expert, n_used, wg, wu, wd, layer)
    return _combine(h, y_sorted, pos_tiles, wts.T, mods, 5, rows, seq, n_batch)


def _rope_tables(n_batch, seq, total_rows):
    pos = jnp.arange(seq)
    row = (pos // GRID_W).astype(F32)
    col = (pos % GRID_W).astype(F32)
    inv = ROPE_THETA ** (-jnp.arange(0, ROPE_AXIS_DIM, 2, dtype=F32) / ROPE_AXIS_DIM)
    ar = row[:, None] * inv
    ac = col[:, None] * inv
    cos = jnp.concatenate([jnp.cos(ar), jnp.cos(ar), jnp.cos(ac), jnp.cos(ac)], axis=-1)
    sin = jnp.concatenate([-jnp.sin(ar), jnp.sin(ar), -jnp.sin(ac), jnp.sin(ac)], axis=-1)
    n_ctx_rows = total_rows - n_batch * seq
    cos = jnp.concatenate([jnp.tile(cos, (n_batch, 1)), jnp.ones((n_ctx_rows, HEAD_DIM), F32)], axis=0)
    sin = jnp.concatenate([jnp.tile(sin, (n_batch, 1)), jnp.zeros((n_ctx_rows, HEAD_DIM), F32)], axis=0)
    return cos, sin


def kernel(x, c, ctx, c_ctx, ada_down, ada_up, ada_bias, norm1_g, norm2_g, swa_w_qkv, swa_w_o, swa_q_gain, swa_k_gain, swa_sink, fnet_w_o, nat_w_qkv, nat_w_o, nat_q_gain, nat_k_gain, nat_rpb, moe_router, moe_bias, moe_w_gate, moe_w_up, moe_w_down, shared_w_gate, shared_w_up, shared_w_down):
    n_batch, seq, _ = x.shape
    ctx_len = ctx.shape[1]
    lat_rows = n_batch * seq
    all_rows = lat_rows + n_batch * ctx_len
    assert n_batch + 1 <= COND_ROWS and seq % ROW_TILE == 0 and lat_rows % ROW_TILE == 0 and all_rows % ROW_TILE == 0
    assert seq // GRID_W >= NAT_KH and seq % NAT_BLOCK == 0 and ctx_len == NAT_BLOCK

    h = jnp.concatenate([x.reshape(lat_rows, D_MODEL), ctx.reshape(n_batch * ctx_len, D_MODEL)], axis=0)
    cond = jnp.concatenate([c, c_ctx[None, :], jnp.zeros((COND_ROWS - n_batch - 1, D_MODEL), F32)], axis=0)
    mods_all = _ada_mods(cond, ada_down, ada_up, ada_bias).reshape(DEPTH, COND_ROWS, N_MOD, 1, D_MODEL)
    cos, sin = _rope_tables(n_batch, seq, all_rows)
    ones_row = jnp.ones((N_HEADS * HEAD_DIM,), F32)

    slot = [0, 0, 0]
    for layer in range(DEPTH):
        kind = layer % 3
        s = slot[kind]
        slot[kind] += 1
        ctx_out = layer < DEPTH - 1
        rows = all_rows if ctx_out else lat_rows
        mods = mods_all[layer]
        common = dict(n_batch=n_batch, seq=seq, ctx_len=ctx_len, out_rows=rows)
        xn = _norm_mod(h, norm1_g[layer], mods, 0, 1, all_rows, seq, n_batch)
        if kind == 0:
            nq, nkv = N_HEADS * HEAD_DIM, N_KV_HEADS * HEAD_DIM
            gain = jnp.concatenate([jnp.tile(swa_q_gain[s], N_HEADS), jnp.tile(swa_k_gain[s], N_KV_HEADS),
                                    ones_row[:nkv]])[None, :]
            qkv = _qkv_proj(xn, swa_w_qkv, s, gain, cos, sin, nq + nkv, rope=True)
            att = dict(kind="swa", k_col0=N_HEADS, v_col0=N_HEADS + N_KV_HEADS, group=GQA_GROUP,
                       sink=swa_sink[s], **common)
            o = _attention(qkv, latent=True, **att)
            if ctx_out:
                o = _attention(qkv, latent=False, prev=o, **att)
            w_o = swa_w_o
        elif kind == 1:
            o = _fourier_mix(xn, all_rows, n_batch, seq, ctx_len, ctx_out)
            w_o = fnet_w_o
        else:
            width = N_HEADS * HEAD_DIM
            gain = jnp.concatenate([jnp.tile(nat_q_gain[s], N_HEADS), jnp.tile(nat_k_gain[s], N_HEADS),
                                    ones_row])[None, :]
            qkv = _qkv_proj(xn, nat_w_qkv, s, gain, cos, sin, 2 * width, rope=False)
            att = dict(kind="nat", k_col0=N_HEADS, v_col0=2 * N_HEADS, group=1, **common)
            o = _attention(qkv, latent=True, bias=_nat_bias_table(nat_rpb[s]), **att)
            if ctx_out:
                o = _attention(qkv, latent=False, prev=o, **att)
            w_o = nat_w_o
        h = _mm_residual(o, w_o, s, h, mods, 2, rows, seq, n_batch)
        h = _moe(h, norm2_g[layer], mods, layer, rows, seq, n_batch, moe_router[layer], moe_bias[layer],
                 moe_w_gate, moe_w_up, moe_w_down, shared_w_gate, shared_w_up, shared_w_down)
    return h[:lat_rows].reshape(n_batch, seq, D_MODEL)
```

```python
import functools

import numpy as np
import jax
import jax.numpy as jnp
from jax import lax
from jax.experimental import pallas as pl
from jax.experimental.pallas import tpu as pltpu

F32 = jnp.float32
BF16 = jnp.bfloat16
U32 = jnp.uint32
I32 = jnp.int32

D_MODEL = 4096
DEPTH = 4
GRID_W = 64
HEAD_DIM = 128
N_HEADS = 32
N_KV_HEADS = 8
GQA_GROUP = N_HEADS // N_KV_HEADS
WINDOW = 128
ROPE_THETA = 10000.0
ROPE_AXIS_DIM = HEAD_DIM // 2
NAT_KH = 8
NAT_KW = 16
NAT_ROWS_PER_BLOCK = 4
NAT_BLOCK = NAT_ROWS_PER_BLOCK * GRID_W
FOURIER_GROUPS = 8
FOURIER_GROUP_DIM = D_MODEL // FOURIER_GROUPS
N_EXPERTS = 32
TOP_K = 4
N_EXPERT_GROUPS = 4
GROUP_SIZE = N_EXPERTS // N_EXPERT_GROUPS
TOPK_GROUPS = 2
EXPERT_HIDDEN = 256
SHARED_HIDDEN = 1024
ROUTED_SCALE = 2.5
ADA_RANK = 256
N_MOD = 6
EPS = 1e-6
MASKED = -1e30

VMEM_LIMIT_BYTES = 56 * 1024 * 1024
ROW_TILE = 1024
COL_TILE = 512
EXPERT_TILE = 256
TOKEN_TILE = 256
HALF_D = D_MODEL // 2
LANES = 128
SLAB = HALF_D // LANES
COND_ROWS = 16
DMA_UNROLL = 8
N_DISPATCH_BUFS = 3
ATTN_HEADS_PER_STEP = 8


def _params(*sem, **kw):
    return pltpu.CompilerParams(dimension_semantics=sem, vmem_limit_bytes=VMEM_LIMIT_BYTES, **kw)


def _dot(a, b):
    return jnp.dot(a, b, preferred_element_type=F32)


def _dot_nt(a, b):
    return lax.dot_general(a, b, (((1,), (1,)), ((), ())), preferred_element_type=F32)


def _silu(x):
    return x * (1.0 / (1.0 + jnp.exp(-x)))


def _pack_halves(lo, hi):
    lo_bits = lax.bitcast_convert_type(lo.astype(BF16).astype(F32), U32)
    hi_bits = lax.bitcast_convert_type(hi.astype(BF16).astype(F32), U32)
    return (lo_bits >> 16) | hi_bits


def _unpack_halves(word):
    lo = lax.bitcast_convert_type(word << 16, F32)
    hi = lax.bitcast_convert_type(word & jnp.uint32(0xFFFF0000), F32)
    return lo, hi


def _slab_rows(first_token, n_tokens, part):
    return pl.ds(first_token * SLAB + part, n_tokens, stride=SLAB)


def _store_slabs(ref, packed, first_token=0):
    n = packed.shape[0]
    for part in range(SLAB):
        ref[_slab_rows(first_token, n, part), :] = packed[:, part * LANES:(part + 1) * LANES]


def _load_slabs(ref, n, first_token=0):
    return jnp.concatenate([ref[_slab_rows(first_token, n, part), :] for part in range(SLAB)], axis=1)


def _ada_kernel(cond_ref, down_ref, up_ref, bias_ref, o_ref):
    c = cond_ref[...]
    z = _dot(_silu(c).astype(BF16), down_ref[0].astype(BF16))
    o_ref[0] = _dot(z.astype(BF16), up_ref[0].astype(BF16)) + bias_ref[0]


def _ada_mods(cond, ada_down, ada_up, ada_bias):
    return pl.pallas_call(
        _ada_kernel,
        grid=(DEPTH, N_MOD),
        in_specs=[
            pl.BlockSpec((COND_ROWS, D_MODEL), lambda l, j: (0, 0)),
            pl.BlockSpec((1, D_MODEL, ADA_RANK), lambda l, j: (l, 0, 0)),
            pl.BlockSpec((1, ADA_RANK, D_MODEL), lambda l, j: (l, 0, j)),
            pl.BlockSpec((1, 1, D_MODEL), lambda l, j: (l, 0, j)),
        ],
        out_specs=pl.BlockSpec((1, COND_ROWS, D_MODEL), lambda l, j: (l, 0, j)),
        out_shape=jax.ShapeDtypeStruct((DEPTH, COND_ROWS, N_MOD * D_MODEL), F32),
        compiler_params=_params("arbitrary", "arbitrary"),
        name="ada_mods",
    )(cond, ada_down, ada_up, ada_bias[:, None, :])


def _norm_mod_tile(h_ref, g_ref, shift_ref, scale_ref):
    x = h_ref[...]
    ms = jnp.mean(x * x, axis=-1, keepdims=True)
    y = x * lax.rsqrt(ms + EPS) * g_ref[...]
    return y * (1.0 + scale_ref[0, 0]) + shift_ref[0, 0]


def _norm_mod_kernel(h_ref, g_ref, shift_ref, scale_ref, o_ref):
    o_ref[...] = _norm_mod_tile(h_ref, g_ref, shift_ref, scale_ref).astype(BF16)


def _split_bf16(x):
    hi = x.astype(BF16)
    lo = (x - hi.astype(F32)).astype(BF16)
    return hi, lo


def _first_index_of_max(v, idx, n):
    m = jnp.max(v, axis=0, keepdims=True)
    i = jnp.min(jnp.where(v == m, idx, n), axis=0, keepdims=True)
    return m, i


def _norm_route_kernel(h_ref, g_ref, shift_ref, scale_ref, wr_ref, eb_ref,
                       o_ref, xw_ref, idx_ref, wts_ref, rank_ref, cnt_ref, run_ref):
    @pl.when(pl.program_id(0) == 0)
    def _():
        run_ref[...] = jnp.zeros_like(run_ref)

    xn = _norm_mod_tile(h_ref, g_ref, shift_ref, scale_ref)
    x_hi = xn.astype(BF16)
    hi_f32 = x_hi.astype(F32)
    o_ref[...] = x_hi
    hi_bits = lax.bitcast_convert_type(hi_f32, U32)
    _store_slabs(xw_ref, (hi_bits[:, :HALF_D] >> 16) | hi_bits[:, HALF_D:])
    x_lo = (xn - hi_f32).astype(BF16)
    prod_t = (_dot(x_hi, wr_ref[...]) + _dot(x_lo, wr_ref[...])).T
    lt = prod_t[:N_EXPERTS] + prod_t[N_EXPERTS:2 * N_EXPERTS]
    scores = 1.0 / (1.0 + jnp.exp(-lt))
    sel = scores + eb_ref[...]
    n_tok = sel.shape[1]
    sub = lax.broadcasted_iota(I32, (GROUP_SIZE, n_tok), 0).astype(F32)
    grp_scores = []
    for g in range(N_EXPERT_GROUPS):
        sg = sel[g * GROUP_SIZE:(g + 1) * GROUP_SIZE]
        m1, i1 = _first_index_of_max(sg, sub, float(GROUP_SIZE))
        m2 = jnp.max(jnp.where(sub == i1, -jnp.inf, sg), axis=0, keepdims=True)
        grp_scores.append(m1 + m2)
    eidx_i = lax.broadcasted_iota(I32, sel.shape, 0)
    eidx = eidx_i.astype(F32)
    egrp = (eidx_i // GROUP_SIZE).astype(F32)
    best = jnp.full((1, n_tok), -jnp.inf, F32)
    for g in range(N_EXPERT_GROUPS):
        best = jnp.maximum(best, grp_scores[g])
    g1 = jnp.full((1, n_tok), float(N_EXPERT_GROUPS), F32)
    for g in reversed(range(N_EXPERT_GROUPS)):
        g1 = jnp.where(grp_scores[g] == best, float(g), g1)
    second = jnp.full((1, n_tok), -jnp.inf, F32)
    for g in range(N_EXPERT_GROUPS):
        second = jnp.maximum(second, jnp.where(g1 == float(g), -jnp.inf, grp_scores[g]))
    g2 = jnp.full((1, n_tok), float(N_EXPERT_GROUPS), F32)
    for g in reversed(range(N_EXPERT_GROUPS)):
        g2 = jnp.where((grp_scores[g] == second) & (g1 != float(g)), float(g), g2)
    masked = jnp.where((egrp == g1) | (egrp == g2), sel, -jnp.inf)
    picks, weights, hits = [], [], []
    for _ in range(TOP_K):
        _, ik = _first_index_of_max(masked, eidx, float(N_EXPERTS))
        hit = eidx == ik
        weights.append(jnp.sum(jnp.where(hit, scores, 0.0), axis=0, keepdims=True))
        picks.append(ik)
        hits.append(hit)
        masked = jnp.where(hit, -jnp.inf, masked)
    w = jnp.concatenate(weights, axis=0)
    idx_ref[...] = jnp.concatenate(picks, axis=0).astype(I32)
    wts_ref[...] = w / jnp.sum(w, axis=0, keepdims=True) * ROUTED_SCALE
    member = jnp.zeros(sel.shape, F32)
    for hit in hits:
        member = jnp.where(hit, 1.0, member)
    earlier = lax.broadcasted_iota(I32, (n_tok, n_tok), 0) < lax.broadcasted_iota(I32, (n_tok, n_tok), 1)
    before = run_ref[...] + _dot(member.astype(BF16), jnp.where(earlier, 1.0, 0.0).astype(BF16))
    rank_ref[...] = jnp.concatenate(
        [jnp.sum(jnp.where(hit, before, 0.0), axis=0, keepdims=True) for hit in hits], axis=0).astype(I32)
    run_ref[...] = run_ref[...] + jnp.sum(member, axis=1, keepdims=True)
    cnt_ref[...] = run_ref[...]


def _seg_of_tile(i, tm, seq, n_batch):
    return jnp.minimum((i * tm) // seq, n_batch)


def _norm_mod(h, g, mods, shift_k, scale_k, rows, seq, n_batch, router=None):
    tm = TOKEN_TILE
    seg = functools.partial(_seg_of_tile, tm=tm, seq=seq, n_batch=n_batch)
    in_specs = [
        pl.BlockSpec((tm, D_MODEL), lambda i: (i, 0)),
        pl.BlockSpec((1, D_MODEL), lambda i: (0, 0)),
        pl.BlockSpec((1, 1, 1, D_MODEL), lambda i: (seg(i), shift_k, 0, 0)),
        pl.BlockSpec((1, 1, 1, D_MODEL), lambda i: (seg(i), scale_k, 0, 0)),
    ]
    xn_spec = pl.BlockSpec((tm, D_MODEL), lambda i: (i, 0))
    xn_shape = jax.ShapeDtypeStruct((rows, D_MODEL), BF16)
    if router is None:
        return pl.pallas_call(
            _norm_mod_kernel, grid=(rows // tm,), in_specs=in_specs, out_specs=xn_spec, out_shape=xn_shape,
            compiler_params=_params("arbitrary"), name="norm_mod",
        )(h, g.reshape(1, D_MODEL), mods, mods)
    w_router, e_bias = router
    w_hi, w_lo = _split_bf16(w_router)
    w_pad = jnp.pad(jnp.concatenate([w_hi, w_lo], axis=1), ((0, 0), (0, LANES - 2 * N_EXPERTS)))
    in_specs += [
        pl.BlockSpec((D_MODEL, 128), lambda i: (0, 0)),
        pl.BlockSpec((N_EXPERTS, 1), lambda i: (0, 0)),
    ]
    per_tok = pl.BlockSpec((TOP_K, tm), lambda i: (0, i))
    return pl.pallas_call(
        _norm_route_kernel, grid=(rows // tm,), in_specs=in_specs,
        out_specs=[xn_spec, pl.BlockSpec((tm * SLAB, LANES), lambda i: (i, 0)), per_tok, per_tok, per_tok,
                   pl.BlockSpec((N_EXPERTS, 1), lambda i: (0, 0))],
        out_shape=[xn_shape, jax.ShapeDtypeStruct((rows * SLAB, LANES), U32),
                   jax.ShapeDtypeStruct((TOP_K, rows), I32), jax.ShapeDtypeStruct((TOP_K, rows), F32),
                   jax.ShapeDtypeStruct((TOP_K, rows), I32), jax.ShapeDtypeStruct((N_EXPERTS, 1), F32)],
        scratch_shapes=[pltpu.VMEM((N_EXPERTS, 1), F32)],
        compiler_params=_params("arbitrary"), name="norm_route",
    )(h, g.reshape(1, D_MODEL), mods, mods, w_pad, e_bias.reshape(N_EXPERTS, 1))


def _cast_weight_once(w_ref, w_scr):
    @pl.when(pl.program_id(1) == 0)
    def _():
        w_scr[...] = w_ref[0].astype(BF16)


def _qkv_kernel(a_ref, w_ref, gain_ref, cos_ref, sin_ref, o_ref, w_scr, *, n_qk_tiles, rope):
    _cast_weight_once(w_ref, w_scr)
    j = pl.program_id(0)
    tm, tn = o_ref.shape
    part = tn if rope else 2 * HEAD_DIM
    whole = _dot(a_ref[...], w_scr[...]) if rope else None

    @pl.when(j < n_qk_tiles)
    def _():
        lane = lax.broadcasted_iota(I32, (tm, HEAD_DIM), 1)
        quarter = ROPE_AXIS_DIM // 2
        first_half = (lane % ROPE_AXIS_DIM) < quarter
        for p0 in range(0, tn, part):
            acc = whole if rope else _dot(a_ref[...], w_scr[:, p0:p0 + part])
            outs = []
            for h0 in range(0, part, HEAD_DIM):
                x = acc[:, h0:h0 + HEAD_DIM]
                ms = jnp.mean(x * x, axis=-1, keepdims=True)
                y = x * lax.rsqrt(ms + EPS) * gain_ref[:, p0 + h0:p0 + h0 + HEAD_DIM]
                if rope:
                    partner = jnp.where(first_half, pltpu.roll(y, HEAD_DIM - quarter, 1), pltpu.roll(y, quarter, 1))
                    y = y * cos_ref[...] + partner * sin_ref[...]
                outs.append(y.astype(BF16))
            o_ref[:, p0:p0 + part] = jnp.concatenate(outs, axis=1)

    @pl.when(j >= n_qk_tiles)
    def _():
        o_ref[...] = (whole if rope else _dot(a_ref[...], w_scr[...])).astype(BF16)


def _qkv_proj(xn, w_stack, w_slot, gain_row, cos, sin, n_qk_cols, rope):
    rows, width = xn.shape[0], w_stack.shape[2]
    tm, tn = ROW_TILE, COL_TILE
    return pl.pallas_call(
        functools.partial(_qkv_kernel, n_qk_tiles=n_qk_cols // tn, rope=rope),
        grid=(width // tn, rows // tm),
        in_specs=[
            pl.BlockSpec((tm, D_MODEL), lambda j, i: (i, 0)),
            pl.BlockSpec((1, D_MODEL, tn), lambda j, i: (w_slot, 0, j)),
            pl.BlockSpec((1, tn), lambda j, i: (0, j)),
            pl.BlockSpec((tm, HEAD_DIM), lambda j, i: (i, 0)),
            pl.BlockSpec((tm, HEAD_DIM), lambda j, i: (i, 0)),
        ],
        out_specs=pl.BlockSpec((tm, tn), lambda j, i: (i, j)),
        out_shape=jax.ShapeDtypeStruct((rows, width), BF16),
        scratch_shapes=[pltpu.VMEM((D_MODEL, tn), BF16)],
        compiler_params=_params("arbitrary", "arbitrary"),
        name="qkv_proj",
    )(xn, w_stack, gain_row, cos, sin)


def _mm_res_kernel(a_ref, w_ref, h_ref, gate_ref, o_ref, w_scr):
    _cast_weight_once(w_ref, w_scr)
    o_ref[...] = h_ref[...] + gate_ref[0, 0] * _dot(a_ref[...], w_scr[...])


def _mm_residual(a, w_stack, w_slot, h, mods, gate_k, rows, seq, n_batch):
    _, k_dim, width = w_stack.shape
    tm = ROW_TILE
    tn = COL_TILE if k_dim >= D_MODEL else 2 * COL_TILE
    seg = functools.partial(_seg_of_tile, tm=tm, seq=seq, n_batch=n_batch)
    return pl.pallas_call(
        _mm_res_kernel,
        grid=(width // tn, rows // tm),
        in_specs=[
            pl.BlockSpec((tm, k_dim), lambda j, i: (i, 0)),
            pl.BlockSpec((1, k_dim, tn), lambda j, i: (w_slot, 0, j)),
            pl.BlockSpec((tm, tn), lambda j, i: (i, j)),
            pl.BlockSpec((1, 1, 1, tn), lambda j, i: (seg(i), gate_k, 0, j)),
        ],
        out_specs=pl.BlockSpec((tm, tn), lambda j, i: (i, j)),
        out_shape=jax.ShapeDtypeStruct((rows, width), F32),
        scratch_shapes=[pltpu.VMEM((k_dim, tn), BF16)],
        compiler_params=_params("arbitrary", "arbitrary"),
        name="mm_residual",
    )(a, w_stack, h, mods)


def _attn_kernel(*refs, n_local, group, block, kind, has_sink, has_prev, n_blocks, heads):
    refs = list(refs)
    sink_ref = refs.pop(0) if has_sink else None
    q_ref = refs.pop(0)
    k_loc = [refs.pop(0) for _ in range(n_local)]
    v_loc = [refs.pop(0) for _ in range(n_local)]
    kc_ref, vc_ref = refs.pop(0), refs.pop(0)
    bias_ref = refs.pop(0) if kind == "nat" else None
    if has_prev:
        refs.pop(0)
    o_ref = refs.pop(0)
    head0 = pl.program_id(1) * heads
    i = pl.program_id(2)
    scale = HEAD_DIM ** -0.5
    n_rows = group * block

    valid = None
    if n_local:
        row = lax.broadcasted_iota(I32, (n_rows, 1), 0)
        col = lax.broadcasted_iota(I32, (1, n_local * block), 1)
        if kind == "swa":
            rel = col - block - (row % block)
            lo = jnp.where(i == 0, block, 0)
            hi = jnp.where(i == n_blocks - 1, 2 * block, 3 * block)
            valid = (rel >= -WINDOW) & (rel <= WINDOW) & (col >= lo) & (col < hi)
        else:
            q_row = NAT_ROWS_PER_BLOCK * i + row // GRID_W
            k_row = NAT_ROWS_PER_BLOCK * (i - 1) + col // GRID_W
            r0 = jnp.clip(q_row - NAT_KH // 2, 0, n_blocks * NAT_ROWS_PER_BLOCK - NAT_KH)
            valid = (k_row >= r0) & (k_row < r0 + NAT_KH)
    g_of_row = lax.broadcasted_iota(I32, (n_rows, 1), 0) // block if has_sink else None

    outs = []
    for hd in range(heads):
        kv_cols = slice(hd * HEAD_DIM, (hd + 1) * HEAD_DIM)
        if group > 1:
            q = jnp.concatenate([q_ref[:, (hd * group + g) * HEAD_DIM:(hd * group + g + 1) * HEAD_DIM]
                                 for g in range(group)], axis=0)
        else:
            q = q_ref[:, kv_cols]
        s_ctx = _dot_nt(q, kc_ref[:, kv_cols]) * scale
        m = jnp.max(s_ctx, axis=-1, keepdims=True)
        if n_local:
            k = jnp.concatenate([r[:, kv_cols] for r in k_loc], axis=0)
            v = jnp.concatenate([r[:, kv_cols] for r in v_loc], axis=0)
            s_loc = _dot_nt(q, k) * scale
            if kind == "nat":
                s_loc = s_loc + bias_ref[hd]
            s_loc = jnp.where(valid, s_loc, MASKED)
            m = jnp.maximum(m, jnp.max(s_loc, axis=-1, keepdims=True))
        if has_sink:
            sink = jnp.zeros((n_rows, 1), F32)
            for g in range(group):
                sink = jnp.where(g_of_row == g, sink_ref[(head0 + hd) * group + g], sink)
            m = jnp.maximum(m, sink)
        p_ctx = jnp.exp(s_ctx - m)
        denom = jnp.sum(p_ctx, axis=-1, keepdims=True)
        o = _dot(p_ctx.astype(BF16), vc_ref[:, kv_cols])
        if n_local:
            p_loc = jnp.exp(s_loc - m)
            denom = denom + jnp.sum(p_loc, axis=-1, keepdims=True)
            o = o + _dot(p_loc.astype(BF16), v)
        if has_sink:
            denom = denom + jnp.exp(sink - m)
        o = (o * (1.0 / denom)).astype(BF16)
        outs += [o[g * block:(g + 1) * block] for g in range(group)]
    o_ref[...] = jnp.concatenate(outs, axis=1) if len(outs) > 1 else outs[0]


def _attention(qkv, *, kind, latent, n_batch, seq, ctx_len, k_col0, v_col0, group, out_rows,
               sink=None, bias=None, prev=None):
    heads = ATTN_HEADS_PER_STEP
    n_kv = v_col0 - k_col0
    assert n_kv % heads == 0 and k_col0 % heads == 0 and v_col0 % heads == 0
    n_steps_h = n_kv // heads
    kc0, vc0 = k_col0 // heads, v_col0 // heads
    block = (WINDOW if kind == "swa" else NAT_BLOCK) if latent else ctx_len
    n_blocks = seq // block if latent else 1
    n_local = 3 if latent else 0
    ctx_blk0 = n_batch * seq // ctx_len
    q_blk0 = 0 if latent else n_batch * seq // block
    has_sink = sink is not None
    nat_order = kind == "nat"

    def ids(a, b, c):
        return (b, a, c) if nat_order else (a, b, c)

    def q_map(a, b, c):
        bb, hh, ii = ids(a, b, c)
        return (q_blk0 + bb * n_blocks + ii, hh)

    def loc_map(delta, col0):
        def f(a, b, c):
            bb, hh, ii = ids(a, b, c)
            return (bb * n_blocks + jnp.clip(ii + delta, 0, n_blocks - 1), col0 + hh)
        return f

    def ctx_map(col0):
        def f(a, b, c):
            bb, hh, _ = ids(a, b, c)
            return (ctx_blk0 + bb, col0 + hh)
        return f

    in_specs, args = [], []
    if has_sink:
        in_specs.append(pl.BlockSpec(memory_space=pltpu.SMEM))
        args.append(sink.astype(F32))
    in_specs.append(pl.BlockSpec((block, heads * group * HEAD_DIM), q_map))
    args.append(qkv)
    for col0 in ((kc0, vc0) if latent else ()):
        for delta in (-1, 0, 1):
            in_specs.append(pl.BlockSpec((block, heads * HEAD_DIM), loc_map(delta, col0)))
            args.append(qkv)
    for col0 in (kc0, vc0):
        in_specs.append(pl.BlockSpec((ctx_len, heads * HEAD_DIM), ctx_map(col0)))
        args.append(qkv)
    use_bias = nat_order and latent
    if use_bias:
        in_specs.append(pl.BlockSpec((heads, block, 3 * block), lambda a, b, c: (a, 0, 0)))
        args.append(bias)
    aliases = {}
    if prev is not None:
        aliases = {len(args): 0}
        in_specs.append(pl.BlockSpec(memory_space=pl.ANY))
        args.append(prev)
    grid = (n_steps_h, n_batch, n_blocks) if nat_order else (n_batch, n_steps_h, n_blocks)
    kern = functools.partial(_attn_kernel, n_local=n_local, group=group, block=block,
                             kind=("nat" if use_bias else kind if latent else "ctx"),
                             has_sink=has_sink, has_prev=prev is not None, n_blocks=n_blocks, heads=heads)
    return pl.pallas_call(
        kern, grid=grid, in_specs=in_specs,
        out_specs=pl.BlockSpec((block, heads * group * HEAD_DIM), q_map),
        out_shape=jax.ShapeDtypeStruct((out_rows, n_kv * group * HEAD_DIM), BF16),
        input_output_aliases=aliases,
        compiler_params=_params("arbitrary", "arbitrary", "arbitrary"),
        name=f"attn_{kind}_{'latent' if latent else 'ctx'}",
    )(*args)


def _nat_bias_table(rpb):
    q_c = np.arange(GRID_W)[:, None]
    k_c = np.arange(GRID_W)[None, :]
    d_col = np.clip(k_c - q_c + NAT_KW - 1, 0, 2 * NAT_KW - 2)
    c0 = np.clip(q_c - NAT_KW // 2, 0, GRID_W - NAT_KW)
    col_valid = (k_c >= c0) & (k_c < c0 + NAT_KW)
    onehot = (np.arange(2 * NAT_KW - 1)[:, None, None] == d_col[None]).astype(np.float32)
    by_col = jnp.einsum("hrd,dqk->hrqk", rpb.astype(F32), jnp.asarray(onehot), precision=lax.Precision.HIGHEST)
    by_col = jnp.where(col_valid[None, None], by_col, MASKED)
    n_key_rows = 3 * NAT_ROWS_PER_BLOCK
    rows = []
    for q_r in range(NAT_ROWS_PER_BLOCK):
        d_row = [k_r - NAT_ROWS_PER_BLOCK - q_r + NAT_KH - 1 for k_r in range(n_key_rows)]
        rows.append(jnp.concatenate([by_col[:, d] for d in d_row], axis=-1))
    return jnp.concatenate(rows, axis=1)


def _dft_chan_kernel(a_ref, w_ref, o_ref):
    o_ref[0] = _dot(a_ref[...], w_ref[0]).astype(BF16)


def _dft_pos_kernel(cn_ref, sn_ref, xc_ref, xs_ref, *rest, scale):
    o_ref = rest[-1]
    o_ref[...] = ((_dot(cn_ref[...], xc_ref[0]) - _dot(sn_ref[...], xs_ref[0])) * scale).astype(BF16)


def _dft_tables(n):
    k = jnp.arange(n, dtype=I32)
    ang = ((k[:, None] * k[None, :]) % n).astype(F32) * (2.0 * np.pi / n)
    return jnp.cos(ang).astype(BF16), jnp.sin(ang).astype(BF16)


def _fourier_mix(xn, rows, n_batch, seq, ctx_len, ctx_out):
    gd = FOURIER_GROUP_DIM
    cd, sd = _dft_tables(gd)
    chan = jnp.stack([cd, sd])
    tm = ROW_TILE
    xcs = pl.pallas_call(
        _dft_chan_kernel,
        grid=(rows // tm, FOURIER_GROUPS, 2),
        in_specs=[pl.BlockSpec((tm, gd), lambda i, g, t: (i, g)),
                  pl.BlockSpec((1, gd, gd), lambda i, g, t: (t, 0, 0))],
        out_specs=pl.BlockSpec((1, tm, gd), lambda i, g, t: (t, i, g)),
        out_shape=jax.ShapeDtypeStruct((2, rows, D_MODEL), BF16),
        compiler_params=_params("arbitrary", "arbitrary", "arbitrary"),
        name="dft_channels",
    )(xn, chan)

    def pos_call(n, tmo, row_blk0, in_blk0, out_rows, prev):
        cn, sn = _dft_tables(n)
        tn = COL_TILE
        per = n // tmo
        in_specs = [
            pl.BlockSpec((tmo, n), lambda b, i, j: (i, 0)),
            pl.BlockSpec((tmo, n), lambda b, i, j: (i, 0)),
            pl.BlockSpec((1, n, tn), lambda b, i, j: (0, in_blk0 + b, j)),
            pl.BlockSpec((1, n, tn), lambda b, i, j: (1, in_blk0 + b, j)),
        ]
        args = [cn, sn, xcs, xcs]
        aliases = {}
        if prev is not None:
            in_specs.append(pl.BlockSpec(memory_space=pl.ANY))
            args.append(prev)
            aliases = {4: 0}
        return pl.pallas_call(
            functools.partial(_dft_pos_kernel, scale=float((n * gd) ** -0.5)),
            grid=(n_batch, per, D_MODEL // tn),
            in_specs=in_specs,
            out_specs=pl.BlockSpec((tmo, tn), lambda b, i, j: (row_blk0 + b * per + i, j)),
            out_shape=jax.ShapeDtypeStruct((out_rows, D_MODEL), BF16),
            input_output_aliases=aliases,
            compiler_params=_params("arbitrary", "arbitrary", "arbitrary"),
            name=f"dft_positions_{n}",
        )(*args)

    z = pos_call(seq, 512, 0, 0, rows, None)
    if ctx_out:
        z = pos_call(ctx_len, ctx_len, n_batch * seq // ctx_len, n_batch * seq // ctx_len, rows, z)
    return z


def _slab_copy(src_ref, src_token, dst_ref, dst_token, sem):
    src = src_ref.at[pl.ds(pl.multiple_of(src_token * SLAB, SLAB), SLAB)]
    dst = dst_ref.at[pl.ds(pl.multiple_of(dst_token * SLAB, SLAB), SLAB)]
    return pltpu.make_async_copy(src, dst, sem)


def _load_slots(pos_ref, pos_smem, sem):
    cp = pltpu.make_async_copy(pos_ref.at[0, 0], pos_smem, sem)
    cp.start()
    cp.wait()


def _for_token_groups(n_tokens, fn):
    def group(g, c):
        for k in range(TOP_K):
            for u in range(DMA_UNROLL):
                fn(k, g * DMA_UNROLL + u)
        return c
    lax.fori_loop(0, n_tokens // DMA_UNROLL, group, 0)


def _dispatch_step(i, n_steps, cnt_ref, off_ref, pos_ref, xw_ref, xs_ref,
                   pos_smem, zero_ref, xbuf_ref, sem_idx, sem_load, sem_rows):
    tm = pos_smem.shape[0] // TOP_K

    @pl.when(i == 0)
    def _():
        zero_ref[...] = jnp.zeros_like(zero_ref)

        def per_expert(e, carry):
            count = cnt_ref[e]
            first = off_ref[e] + count
            n_pad = (count + EXPERT_TILE - 1) // EXPERT_TILE * EXPERT_TILE - count

            def start(r, c):
                _slab_copy(zero_ref, 0, xs_ref, first + r, sem_rows.at[0]).start()
                return c

            def wait(r, c):
                _slab_copy(zero_ref, 0, xs_ref, first + r, sem_rows.at[0]).wait()
                return c

            lax.fori_loop(0, n_pad, start, 0)
            lax.fori_loop(0, n_pad, wait, 0)
            return carry

        lax.fori_loop(0, N_EXPERTS, per_expert, 0)

    def load(step):
        which = step % N_DISPATCH_BUFS
        src = xw_ref.at[pl.ds(pl.multiple_of(step * tm * SLAB, tm * SLAB), tm * SLAB)]
        return pltpu.make_async_copy(src, xbuf_ref.at[which], sem_load.at[which])

    def drain(step):
        which = step % N_DISPATCH_BUFS
        _for_token_groups(tm, lambda k, t: _slab_copy(xbuf_ref.at[which], t, xs_ref, 0, sem_rows.at[which]).wait())

    @pl.when(i == 0)
    def _():
        load(i).start()

    @pl.when(i >= N_DISPATCH_BUFS - 1)
    def _():
        drain(i - (N_DISPATCH_BUFS - 1))

    @pl.when(i + 1 < n_steps)
    def _():
        load(i + 1).start()

    load(i).wait()
    _load_slots(pos_ref, pos_smem, sem_idx)
    cur = i % N_DISPATCH_BUFS
    _for_token_groups(tm, lambda k, t: _slab_copy(
        xbuf_ref.at[cur], t, xs_ref, pos_smem[k * tm + t], sem_rows.at[cur]).start())

    @pl.when(i == n_steps - 1)
    def _():
        for back in reversed(range(N_DISPATCH_BUFS - 1)):
            @pl.when(i - back >= 0)
            def _(back=back):
                drain(i - back)


def _swiglu_dispatch_kernel(cnt_ref, off_ref, a_ref, wg_ref, wu_ref, pos_ref, xw_ref, o_ref, xs_ref,
                            wg_scr, wu_scr, *dispatch_scratch):
    n_inner = pl.num_programs(1)
    step = pl.program_id(0) * n_inner + pl.program_id(1)
    _dispatch_step(step, pl.num_programs(0) * n_inner, cnt_ref, off_ref, pos_ref, xw_ref, xs_ref, *dispatch_scratch)
    _cast_weight_once(wg_ref, wg_scr)
    _cast_weight_once(wu_ref, wu_scr)
    a = a_ref[...]
    o_ref[...] = (_silu(_dot(a, wg_scr[...])) * _dot(a, wu_scr[...])).astype(BF16)


def _swiglu_and_dispatch(a, wg_stack, wu_stack, layer, rows, xw, pos_tiles, counts, offs, n_slots):
    width = wg_stack.shape[2]
    tm, tn, tok = ROW_TILE, COL_TILE // 2, TOKEN_TILE
    n_inner = rows // tm
    assert (width // tn) * n_inner == rows // tok
    w_spec = pl.BlockSpec((1, D_MODEL, tn), lambda j, i, cnt, off: (layer, 0, j))
    grid_spec = pltpu.PrefetchScalarGridSpec(
        num_scalar_prefetch=2,
        grid=(width // tn, n_inner),
        in_specs=[
            pl.BlockSpec((tm, D_MODEL), lambda j, i, cnt, off: (i, 0)),
            w_spec, w_spec,
            pl.BlockSpec((1, 1, TOP_K * tok), lambda j, i, cnt, off: (j * n_inner + i, 0, 0)),
            pl.BlockSpec(memory_space=pl.ANY),
        ],
        out_specs=[pl.BlockSpec((tm, tn), lambda j, i, cnt, off: (i, j)), pl.BlockSpec(memory_space=pl.ANY)],
        scratch_shapes=[
            pltpu.VMEM((D_MODEL, tn), BF16),
            pltpu.VMEM((D_MODEL, tn), BF16),
            pltpu.SMEM((TOP_K * tok,), I32),
            pltpu.VMEM((SLAB, LANES), U32),
            pltpu.VMEM((N_DISPATCH_BUFS, tok * SLAB, LANES), U32),
            pltpu.SemaphoreType.DMA,
            pltpu.SemaphoreType.DMA((N_DISPATCH_BUFS,)),
            pltpu.SemaphoreType.DMA((N_DISPATCH_BUFS,)),
        ],
    )
    return pl.pallas_call(
        _swiglu_dispatch_kernel, grid_spec=grid_spec,
        out_shape=[jax.ShapeDtypeStruct((rows, width), BF16), jax.ShapeDtypeStruct((n_slots * SLAB, LANES), U32)],
        compiler_params=_params("arbitrary", "arbitrary", disable_bounds_checks=True),
        name="shared_swiglu_dispatch",
    )(counts, offs, a, wg_stack, wu_stack, pos_tiles, xw)


def _expert_kernel(te_ref, nu_ref, x_ref, wg_ref, wu_ref, wd_ref, o_ref, wgu_scr, wd_scr):
    j = pl.program_id(0)

    @pl.when(j < nu_ref[0])
    def _():
        @pl.when((j == 0) | (te_ref[j] != te_ref[jnp.maximum(j - 1, 0)]))
        def _():
            wgu_scr[:, :EXPERT_HIDDEN] = wg_ref[0, 0].astype(BF16)
            wgu_scr[:, EXPERT_HIDDEN:] = wu_ref[0, 0].astype(BF16)
            wd_scr[...] = wd_ref[0, 0].astype(BF16)

        x_lo, x_hi = _unpack_halves(_load_slabs(x_ref, EXPERT_TILE))
        x_lo, x_hi = x_lo.astype(BF16), x_hi.astype(BF16)
        gate_up = _dot(x_lo, wgu_scr[:HALF_D]) + _dot(x_hi, wgu_scr[HALF_D:])
        hid = (_silu(gate_up[:, :EXPERT_HIDDEN]) * gate_up[:, EXPERT_HIDDEN:]).astype(BF16)
        _store_slabs(o_ref, _pack_halves(_dot(hid, wd_scr[:, :HALF_D]), _dot(hid, wd_scr[:, HALF_D:])))


def _experts(x_sorted, tile_expert, n_used, wg, wu, wd, layer):
    n_rows = x_sorted.shape[0] // SLAB
    tm = EXPERT_TILE

    def row_map(j, te, nu):
        return (jnp.minimum(j, nu[0] - 1), 0)

    def w_map(j, te, nu):
        return (layer, te[j], 0, 0)

    grid_spec = pltpu.PrefetchScalarGridSpec(
        num_scalar_prefetch=2,
        grid=(n_rows // tm,),
        in_specs=[
            pl.BlockSpec((tm * SLAB, LANES), row_map),
            pl.BlockSpec((1, 1, D_MODEL, EXPERT_HIDDEN), w_map),
            pl.BlockSpec((1, 1, D_MODEL, EXPERT_HIDDEN), w_map),
            pl.BlockSpec((1, 1, EXPERT_HIDDEN, D_MODEL), w_map),
        ],
        out_specs=pl.BlockSpec((tm * SLAB, LANES), row_map),
        scratch_shapes=[
            pltpu.VMEM((D_MODEL, 2 * EXPERT_HIDDEN), BF16),
            pltpu.VMEM((EXPERT_HIDDEN, D_MODEL), BF16),
        ],
    )
    return pl.pallas_call(
        _expert_kernel, grid_spec=grid_spec,
        out_shape=jax.ShapeDtypeStruct((n_rows * SLAB, LANES), U32),
        compiler_params=_params("arbitrary"),
        name="routed_experts",
    )(tile_expert, n_used, x_sorted, wg, wu, wd)


def _combine_kernel(pos_ref, nxt_ref, h_ref, w_ref, gate_ref, y_ref, o_ref, pos_smem, buf_ref, sem_idx, sem_rows):
    i = pl.program_id(0)
    n_steps = pl.num_programs(0)
    tm = h_ref.shape[0]
    cur = i % 2

    def gather(slot_ref, which):
        _load_slots(slot_ref, pos_smem, sem_idx)
        _for_token_groups(tm, lambda k, t: _slab_copy(
            y_ref, pos_smem[k * tm + t], buf_ref.at[which, k], t, sem_rows.at[which]).start())

    @pl.when(i == 0)
    def _():
        gather(pos_ref, 0)

    @pl.when(i + 1 < n_steps)
    def _():
        gather(nxt_ref, 1 - cur)

    _for_token_groups(tm, lambda k, t: _slab_copy(y_ref, 0, buf_ref.at[cur, k], t, sem_rows.at[cur]).wait())
    sub = 8

    def rows_chunk(r, c):
        first = pl.multiple_of(r * sub, sub)
        rs = pl.ds(first, sub)
        w = w_ref[rs, :]
        wk = [jnp.broadcast_to(w[:, k:k + 1], (sub, LANES)) for k in range(TOP_K)]
        for part in range(SLAB):
            terms = []
            for k in range(TOP_K):
                y_lo, y_hi = _unpack_halves(buf_ref[cur, k, _slab_rows(first, sub, part), :])
                terms.append((wk[k] * y_lo, wk[k] * y_hi))
            acc_lo = (terms[0][0] + terms[1][0]) + (terms[2][0] + terms[3][0])
            acc_hi = (terms[0][1] + terms[1][1]) + (terms[2][1] + terms[3][1])
            lo = pl.ds(part * LANES, LANES)
            hi = pl.ds(HALF_D + part * LANES, LANES)
            o_ref[rs, lo] = h_ref[rs, lo] + gate_ref[0, 0, :, lo] * acc_lo
            o_ref[rs, hi] = h_ref[rs, hi] + gate_ref[0, 0, :, hi] * acc_hi
        return c

    lax.fori_loop(0, tm // sub, rows_chunk, 0)


def _combine(h, y_sorted, pos_tiles, wts_t, mods, gate_k, rows, seq, n_batch):
    tm = TOKEN_TILE
    n_steps = rows // tm
    seg = functools.partial(_seg_of_tile, tm=tm, seq=seq, n_batch=n_batch)
    return pl.pallas_call(
        _combine_kernel,
        grid=(n_steps,),
        in_specs=[
            pl.BlockSpec((1, 1, TOP_K * tm), lambda i: (i, 0, 0)),
            pl.BlockSpec((1, 1, TOP_K * tm), lambda i: (jnp.minimum(i + 1, n_steps - 1), 0, 0)),
            pl.BlockSpec((tm, D_MODEL), lambda i: (i, 0)),
            pl.BlockSpec((tm, TOP_K), lambda i: (i, 0)),
            pl.BlockSpec((1, 1, 1, D_MODEL), lambda i: (seg(i), gate_k, 0, 0)),
            pl.BlockSpec(memory_space=pl.ANY),
        ],
        out_specs=pl.BlockSpec((tm, D_MODEL), lambda i: (i, 0)),
        out_shape=jax.ShapeDtypeStruct((rows, D_MODEL), F32),
        scratch_shapes=[
            pltpu.SMEM((TOP_K * tm,), I32),
            pltpu.VMEM((2, TOP_K, tm * SLAB, LANES), U32),
            pltpu.SemaphoreType.DMA,
            pltpu.SemaphoreType.DMA((2,)),
        ],
        compiler_params=_params("arbitrary", disable_bounds_checks=True),
        name="moe_combine",
    )(pos_tiles, pos_tiles, h, wts_t, mods, y_sorted)


def _dispatch_plan(idx, rank, counts_f, rows):
    tm = EXPERT_TILE
    counts = counts_f[:, 0].astype(I32)
    padded = (counts + tm - 1) // tm * tm
    ends = jnp.cumsum(padded)
    offs = ends - padded
    expert = jnp.arange(N_EXPERTS, dtype=I32)[:, None, None]
    pos = rank + jnp.sum(jnp.where(idx[None] == expert, offs[:, None, None], 0), axis=0)
    n_tok_tiles = rows // TOKEN_TILE
    pos_tiles = pos.reshape(TOP_K, n_tok_tiles, TOKEN_TILE).transpose(1, 0, 2).reshape(n_tok_tiles, 1, TOP_K * TOKEN_TILE)
    n_slots = TOP_K * rows + N_EXPERTS * tm
    n_used = ends[-1] // tm
    tile_start = jnp.arange(n_slots // tm, dtype=I32) * tm
    tile_expert = jnp.sum((tile_start[:, None] >= ends[None, :]).astype(I32), axis=1)
    last_expert = jnp.sum((((n_used - 1) * tm) >= ends).astype(I32))
    tile_expert = jnp.where(tile_start < ends[-1], tile_expert, last_expert).astype(I32)
    return pos_tiles, counts, offs.astype(I32), tile_expert, n_used.astype(I32).reshape(1), n_slots


def _moe(h, g2, mods, layer, rows, seq, n_batch, w_router, e_bias, wg, wu, wd, sg, su, sd):
    xn, xw, idx, wts, rank, counts_f = _norm_mod(h, g2, mods, 3, 4, rows, seq, n_batch, router=(w_router, e_bias))
    pos_tiles, counts, offs, tile_expert, n_used, n_slots = _dispatch_plan(idx, rank, counts_f, rows)
    hid_s, x_sorted = _swiglu_and_dispatch(xn, sg, su, layer, rows, xw, pos_tiles, counts, offs, n_slots)
    h = _mm_residual(hid_s, sd, layer, h, mods, 5, rows, seq, n_batch)
    y_sorted = _experts(x_sorted, tile_expert, n_used, wg, wu, wd, layer)
    return _combine(h, y_sorted, pos_tiles, wts.T, mods, 5, rows, seq, n_batch)


def _rope_tables(n_batch, seq, total_rows):
    pos = jnp.arange(seq)
    row = (pos // GRID_W).astype(F32)
    col = (pos % GRID_W).astype(F32)
    inv = ROPE_THETA ** (-jnp.arange(0, ROPE_AXIS_DIM, 2, dtype=F32) / ROPE_AXIS_DIM)
    ar = row[:, None] * inv
    ac = col[:, None] * inv
    cos = jnp.concatenate([jnp.cos(ar), jnp.cos(ar), jnp.cos(ac), jnp.cos(ac)], axis=-1)
    sin = jnp.concatenate([-jnp.sin(ar), jnp.sin(ar), -jnp.sin(ac), jnp.sin(ac)], axis=-1)
    n_ctx_rows = total_rows - n_batch * seq
    cos = jnp.concatenate([jnp.tile(cos, (n_batch, 1)), jnp.ones((n_ctx_rows, HEAD_DIM), F32)], axis=0)
    sin = jnp.concatenate([jnp.tile(sin, (n_batch, 1)), jnp.zeros((n_ctx_rows, HEAD_DIM), F32)], axis=0)
    return cos, sin


def kernel(x, c, ctx, c_ctx, ada_down, ada_up, ada_bias, norm1_g, norm2_g, swa_w_qkv, swa_w_o, swa_q_gain, swa_k_gain, swa_sink, fnet_w_o, nat_w_qkv, nat_w_o, nat_q_gain, nat_k_gain, nat_rpb, moe_router, moe_bias, moe_w_gate, moe_w_up, moe_w_down, shared_w_gate, shared_w_up, shared_w_down):
    n_batch, seq, _ = x.shape
    ctx_len = ctx.shape[1]
    lat_rows = n_batch * seq
    all_rows = lat_rows + n_batch * ctx_len
    assert n_batch + 1 <= COND_ROWS and seq % ROW_TILE == 0 and lat_rows % ROW_TILE == 0 and all_rows % ROW_TILE == 0
    assert seq // GRID_W >= NAT_KH and seq % NAT_BLOCK == 0 and ctx_len == NAT_BLOCK

    h = jnp.concatenate([x.reshape(lat_rows, D_MODEL), ctx.reshape(n_batch * ctx_len, D_MODEL)], axis=0)
    cond = jnp.concatenate([c, c_ctx[None, :], jnp.zeros((COND_ROWS - n_batch - 1, D_MODEL), F32)], axis=0)
    mods_all = _ada_mods(cond, ada_down, ada_up, ada_bias).reshape(DEPTH, COND_ROWS, N_MOD, 1, D_MODEL)
    cos, sin = _rope_tables(n_batch, seq, all_rows)
    ones_row = jnp.ones((N_HEADS * HEAD_DIM,), F32)

    slot = [0, 0, 0]
    for layer in range(DEPTH):
        kind = layer % 3
        s = slot[kind]
        slot[kind] += 1
        ctx_out = layer < DEPTH - 1
        rows = all_rows if ctx_out else lat_rows
        mods = mods_all[layer]
        common = dict(n_batch=n_batch, seq=seq, ctx_len=ctx_len, out_rows=rows)
        xn = _norm_mod(h, norm1_g[layer], mods, 0, 1, all_rows, seq, n_batch)
        if kind == 0:
            nq, nkv = N_HEADS * HEAD_DIM, N_KV_HEADS * HEAD_DIM
            gain = jnp.concatenate([jnp.tile(swa_q_gain[s], N_HEADS), jnp.tile(swa_k_gain[s], N_KV_HEADS),
                                    ones_row[:nkv]])[None, :]
            qkv = _qkv_proj(xn, swa_w_qkv, s, gain, cos, sin, nq + nkv, rope=True)
            att = dict(kind="swa", k_col0=N_HEADS, v_col0=N_HEADS + N_KV_HEADS, group=GQA_GROUP,
                       sink=swa_sink[s], **common)
            o = _attention(qkv, latent=True, **att)
            if ctx_out:
                o = _attention(qkv, latent=False, prev=o, **att)
            w_o = swa_w_o
        elif kind == 1:
            o = _fourier_mix(xn, all_rows, n_batch, seq, ctx_len, ctx_out)
            w_o = fnet_w_o
        else:
            width = N_HEADS * HEAD_DIM
            gain = jnp.concatenate([jnp.tile(nat_q_gain[s], N_HEADS), jnp.tile(nat_k_gain[s], N_HEADS),
                                    ones_row])[None, :]
            qkv = _qkv_proj(xn, nat_w_qkv, s, gain, cos, sin, 2 * width, rope=False)
            att = dict(kind="nat", k_col0=N_HEADS, v_col0=2 * N_HEADS, group=1, **common)
            o = _attention(qkv, latent=True, bias=_nat_bias_table(nat_rpb[s]), **att)
            if ctx_out:
                o = _attention(qkv, latent=False, prev=o, **att)
            w_o = nat_w_o
        h = _mm_residual(o, w_o, s, h, mods, 2, rows, seq, n_batch)
        h = _moe(h, norm2_g[layer], mods, layer, rows, seq, n_batch, moe_router[layer], moe_bias[layer],
                 moe_w_gate, moe_w_up, moe_w_down, shared_w_gate, shared_w_up, shared_w_down)
    return h[:lat_rows].reshape(n_batch, seq, D_MODEL)
```

```python
import functools

import numpy as np
import jax
import jax.numpy as jnp
from jax import lax
from jax.experimental import pallas as pl
from jax.experimental.pallas import tpu as pltpu

F32 = jnp.float32
BF16 = jnp.bfloat16
U32 = jnp.uint32
I32 = jnp.int32

D_MODEL = 4096
DEPTH = 4
GRID_W = 64
HEAD_DIM = 128
N_HEADS = 32
N_KV_HEADS = 8
GQA_GROUP = N_HEADS // N_KV_HEADS
WINDOW = 128
ROPE_THETA = 10000.0
ROPE_AXIS_DIM = HEAD_DIM // 2
NAT_KH = 8
NAT_KW = 16
NAT_ROWS_PER_BLOCK = 4
NAT_BLOCK = NAT_ROWS_PER_BLOCK * GRID_W
FOURIER_GROUPS = 8
FOURIER_GROUP_DIM = D_MODEL // FOURIER_GROUPS
N_EXPERTS = 32
TOP_K = 4
N_EXPERT_GROUPS = 4
GROUP_SIZE = N_EXPERTS // N_EXPERT_GROUPS
TOPK_GROUPS = 2
EXPERT_HIDDEN = 256
SHARED_HIDDEN = 1024
ROUTED_SCALE = 2.5
ADA_RANK = 256
N_MOD = 6
EPS = 1e-6
MASKED = -1e30

VMEM_LIMIT_BYTES = 56 * 1024 * 1024
ROW_TILE = 1024
COL_TILE = 512
EXPERT_TILE = 256
TOKEN_TILE = 256
HALF_D = D_MODEL // 2
LANES = 128
SLAB = HALF_D // LANES
COND_ROWS = 16
DMA_UNROLL = 8
N_DISPATCH_BUFS = 3
ATTN_HEADS_PER_STEP = 8


def _params(*sem, **kw):
    return pltpu.CompilerParams(dimension_semantics=sem, vmem_limit_bytes=VMEM_LIMIT_BYTES, **kw)


def _dot(a, b):
    return jnp.dot(a, b, preferred_element_type=F32)


def _dot_nt(a, b):
    return lax.dot_general(a, b, (((1,), (1,)), ((), ())), preferred_element_type=F32)


def _silu(x):
    return x * (1.0 / (1.0 + jnp.exp(-x)))


def _pack_halves(lo, hi):
    lo_bits = lax.bitcast_convert_type(lo.astype(BF16).astype(F32), U32)
    hi_bits = lax.bitcast_convert_type(hi.astype(BF16).astype(F32), U32)
    return (lo_bits >> 16) | hi_bits


def _unpack_halves(word):
    lo = lax.bitcast_convert_type(word << 16, F32)
    hi = lax.bitcast_convert_type(word & jnp.uint32(0xFFFF0000), F32)
    return lo, hi


def _slab_rows(first_token, n_tokens, part):
    return pl.ds(first_token * SLAB + part, n_tokens, stride=SLAB)


def _store_slabs(ref, packed, first_token=0):
    n = packed.shape[0]
    for part in range(SLAB):
        ref[_slab_rows(first_token, n, part), :] = packed[:, part * LANES:(part + 1) * LANES]


def _load_slabs(ref, n, first_token=0):
    return jnp.concatenate([ref[_slab_rows(first_token, n, part), :] for part in range(SLAB)], axis=1)


def _ada_kernel(cond_ref, down_ref, up_ref, bias_ref, o_ref):
    c = cond_ref[...]
    z = _dot(_silu(c).astype(BF16), down_ref[0].astype(BF16))
    o_ref[0] = _dot(z.astype(BF16), up_ref[0].astype(BF16)) + bias_ref[0]


def _ada_mods(cond, ada_down, ada_up, ada_bias):
    return pl.pallas_call(
        _ada_kernel,
        grid=(DEPTH, N_MOD),
        in_specs=[
            pl.BlockSpec((COND_ROWS, D_MODEL), lambda l, j: (0, 0)),
            pl.BlockSpec((1, D_MODEL, ADA_RANK), lambda l, j: (l, 0, 0)),
            pl.BlockSpec((1, ADA_RANK, D_MODEL), lambda l, j: (l, 0, j)),
            pl.BlockSpec((1, 1, D_MODEL), lambda l, j: (l, 0, j)),
        ],
        out_specs=pl.BlockSpec((1, COND_ROWS, D_MODEL), lambda l, j: (l, 0, j)),
        out_shape=jax.ShapeDtypeStruct((DEPTH, COND_ROWS, N_MOD * D_MODEL), F32),
        compiler_params=_params("arbitrary", "arbitrary"),
        name="ada_mods",
    )(cond, ada_down, ada_up, ada_bias[:, None, :])


def _norm_mod_tile(h_ref, g_ref, shift_ref, scale_ref):
    x = h_ref[...]
    ms = jnp.mean(x * x, axis=-1, keepdims=True)
    y = x * lax.rsqrt(ms + EPS) * g_ref[...]
    return y * (1.0 + scale_ref[0, 0]) + shift_ref[0, 0]


def _norm_mod_kernel(h_ref, g_ref, shift_ref, scale_ref, o_ref):
    o_ref[...] = _norm_mod_tile(h_ref, g_ref, shift_ref, scale_ref).astype(BF16)


def _split_bf16(x):
    hi = x.astype(BF16)
    lo = (x - hi.astype(F32)).astype(BF16)
    return hi, lo


def _first_index_of_max(v, idx, n):
    m = jnp.max(v, axis=0, keepdims=True)
    i = jnp.min(jnp.where(v == m, idx, n), axis=0, keepdims=True)
    return m, i


def _norm_route_kernel(h_ref, g_ref, shift_ref, scale_ref, wr_ref, eb_ref,
                       o_ref, xw_ref, idx_ref, wts_ref, rank_ref, cnt_ref, run_ref):
    @pl.when(pl.program_id(0) == 0)
    def _():
        run_ref[...] = jnp.zeros_like(run_ref)

    xn = _norm_mod_tile(h_ref, g_ref, shift_ref, scale_ref)
    x_hi = xn.astype(BF16)
    hi_f32 = x_hi.astype(F32)
    o_ref[...] = x_hi
    hi_bits = lax.bitcast_convert_type(hi_f32, U32)
    _store_slabs(xw_ref, (hi_bits[:, :HALF_D] >> 16) | hi_bits[:, HALF_D:])
    x_lo = (xn - hi_f32).astype(BF16)
    prod_t = (_dot(x_hi, wr_ref[...]) + _dot(x_lo, wr_ref[...])).T
    lt = prod_t[:N_EXPERTS] + prod_t[N_EXPERTS:2 * N_EXPERTS]
    scores = 1.0 / (1.0 + jnp.exp(-lt))
    sel = scores + eb_ref[...]
    n_tok = sel.shape[1]
    sub = lax.broadcasted_iota(I32, (GROUP_SIZE, n_tok), 0).astype(F32)
    grp_scores = []
    for g in range(N_EXPERT_GROUPS):
        sg = sel[g * GROUP_SIZE:(g + 1) * GROUP_SIZE]
        m1, i1 = _first_index_of_max(sg, sub, float(GROUP_SIZE))
        m2 = jnp.max(jnp.where(sub == i1, -jnp.inf, sg), axis=0, keepdims=True)
        grp_scores.append(m1 + m2)
    eidx_i = lax.broadcasted_iota(I32, sel.shape, 0)
    eidx = eidx_i.astype(F32)
    egrp = (eidx_i // GROUP_SIZE).astype(F32)
    best = jnp.full((1, n_tok), -jnp.inf, F32)
    for g in range(N_EXPERT_GROUPS):
        best = jnp.maximum(best, grp_scores[g])
    g1 = jnp.full((1, n_tok), float(N_EXPERT_GROUPS), F32)
    for g in reversed(range(N_EXPERT_GROUPS)):
        g1 = jnp.where(grp_scores[g] == best, float(g), g1)
    second = jnp.full((1, n_tok), -jnp.inf, F32)
    for g in range(N_EXPERT_GROUPS):
        second = jnp.maximum(second, jnp.where(g1 == float(g), -jnp.inf, grp_scores[g]))
    g2 = jnp.full((1, n_tok), float(N_EXPERT_GROUPS), F32)
    for g in reversed(range(N_EXPERT_GROUPS)):
        g2 = jnp.where((grp_scores[g] == second) & (g1 != float(g)), float(g), g2)
    masked = jnp.where((egrp == g1) | (egrp == g2), sel, -jnp.inf)
    picks, weights, hits = [], [], []
    for _ in range(TOP_K):
        _, ik = _first_index_of_max(masked, eidx, float(N_EXPERTS))
        hit = eidx == ik
        weights.append(jnp.sum(jnp.where(hit, scores, 0.0), axis=0, keepdims=True))
        picks.append(ik)
        hits.append(hit)
        masked = jnp.where(hit, -jnp.inf, masked)
    w = jnp.concatenate(weights, axis=0)
    idx_ref[...] = jnp.concatenate(picks, axis=0).astype(I32)
    wts_ref[...] = w / jnp.sum(w, axis=0, keepdims=True) * ROUTED_SCALE
    member = jnp.zeros(sel.shape, F32)
    for hit in hits:
        member = jnp.where(hit, 1.0, member)
    earlier = lax.broadcasted_iota(I32, (n_tok, n_tok), 0) < lax.broadcasted_iota(I32, (n_tok, n_tok), 1)
    before = run_ref[...] + _dot(member.astype(BF16), jnp.where(earlier, 1.0, 0.0).astype(BF16))
    rank_ref[...] = jnp.concatenate(
        [jnp.sum(jnp.where(hit, before, 0.0), axis=0, keepdims=True) for hit in hits], axis=0).astype(I32)
    run_ref[...] = run_ref[...] + jnp.sum(member, axis=1, keepdims=True)
    cnt_ref[...] = run_ref[...]


def _seg_of_tile(i, tm, seq, n_batch):
    return jnp.minimum((i * tm) // seq, n_batch)


def _norm_mod(h, g, mods, shift_k, scale_k, rows, seq, n_batch, router=None):
    tm = TOKEN_TILE
    seg = functools.partial(_seg_of_tile, tm=tm, seq=seq, n_batch=n_batch)
    in_specs = [
        pl.BlockSpec((tm, D_MODEL), lambda i: (i, 0)),
        pl.BlockSpec((1, D_MODEL), lambda i: (0, 0)),
        pl.BlockSpec((1, 1, 1, D_MODEL), lambda i: (seg(i), shift_k, 0, 0)),
        pl.BlockSpec((1, 1, 1, D_MODEL), lambda i: (seg(i), scale_k, 0, 0)),
    ]
    xn_spec = pl.BlockSpec((tm, D_MODEL), lambda i: (i, 0))
    xn_shape = jax.ShapeDtypeStruct((rows, D_MODEL), BF16)
    if router is None:
        return pl.pallas_call(
            _norm_mod_kernel, grid=(rows // tm,), in_specs=in_specs, out_specs=xn_spec, out_shape=xn_shape,
            compiler_params=_params("arbitrary"), name="norm_mod",
        )(h, g.reshape(1, D_MODEL), mods, mods)
    w_router, e_bias = router
    w_hi, w_lo = _split_bf16(w_router)
    w_pad = jnp.pad(jnp.concatenate([w_hi, w_lo], axis=1), ((0, 0), (0, LANES - 2 * N_EXPERTS)))
    in_specs += [
        pl.BlockSpec((D_MODEL, 128), lambda i: (0, 0)),
        pl.BlockSpec((N_EXPERTS, 1), lambda i: (0, 0)),
    ]
    per_tok = pl.BlockSpec((TOP_K, tm), lambda i: (0, i))
    return pl.pallas_call(
        _norm_route_kernel, grid=(rows // tm,), in_specs=in_specs,
        out_specs=[xn_spec, pl.BlockSpec((tm * SLAB, LANES), lambda i: (i, 0)), per_tok, per_tok, per_tok,
                   pl.BlockSpec((N_EXPERTS, 1), lambda i: (0, 0))],
        out_shape=[xn_shape, jax.ShapeDtypeStruct((rows * SLAB, LANES), U32),
                   jax.ShapeDtypeStruct((TOP_K, rows), I32), jax.ShapeDtypeStruct((TOP_K, rows), F32),
                   jax.ShapeDtypeStruct((TOP_K, rows), I32), jax.ShapeDtypeStruct((N_EXPERTS, 1), F32)],
        scratch_shapes=[pltpu.VMEM((N_EXPERTS, 1), F32)],
        compiler_params=_params("arbitrary"), name="norm_route",
    )(h, g.reshape(1, D_MODEL), mods, mods, w_pad, e_bias.reshape(N_EXPERTS, 1))


def _cast_weight_once(w_ref, w_scr):
    @pl.when(pl.program_id(1) == 0)
    def _():
        w_scr[...] = w_ref[0].astype(BF16)


def _qkv_kernel(a_ref, w_ref, gain_ref, cos_ref, sin_ref, o_ref, w_scr, *, n_qk_tiles, rope):
    _cast_weight_once(w_ref, w_scr)
    j = pl.program_id(0)
    tm, tn = o_ref.shape
    part = tn if rope else 2 * HEAD_DIM
    whole = _dot(a_ref[...], w_scr[...]) if rope else None

    @pl.when(j < n_qk_tiles)
    def _():
        lane = lax.broadcasted_iota(I32, (tm, HEAD_DIM), 1)
        quarter = ROPE_AXIS_DIM // 2
        first_half = (lane % ROPE_AXIS_DIM) < quarter
        for p0 in range(0, tn, part):
            acc = whole if rope else _dot(a_ref[...], w_scr[:, p0:p0 + part])
            outs = []
            for h0 in range(0, part, HEAD_DIM):
                x = acc[:, h0:h0 + HEAD_DIM]
                ms = jnp.mean(x * x, axis=-1, keepdims=True)
                y = x * lax.rsqrt(ms + EPS) * gain_ref[:, p0 + h0:p0 + h0 + HEAD_DIM]
                if rope:
                    partner = jnp.where(first_half, pltpu.roll(y, HEAD_DIM - quarter, 1), pltpu.roll(y, quarter, 1))
                    y = y * cos_ref[...] + partner * sin_ref[...]
                outs.append(y.astype(BF16))
            o_ref[:, p0:p0 + part] = jnp.concatenate(outs, axis=1)

    @pl.when(j >= n_qk_tiles)
    def _():
        o_ref[...] = (whole if rope else _dot(a_ref[...], w_scr[...])).astype(BF16)


def _qkv_proj(xn, w_stack, w_slot, gain_row, cos, sin, n_qk_cols, rope):
    rows, width = xn.shape[0], w_stack.shape[2]
    tm, tn = ROW_TILE, COL_TILE
    return pl.pallas_call(
        functools.partial(_qkv_kernel, n_qk_tiles=n_qk_cols // tn, rope=rope),
        grid=(width // tn, rows // tm),
        in_specs=[
            pl.BlockSpec((tm, D_MODEL), lambda j, i: (i, 0)),
            pl.BlockSpec((1, D_MODEL, tn), lambda j, i: (w_slot, 0, j)),
            pl.BlockSpec((1, tn), lambda j, i: (0, j)),
            pl.BlockSpec((tm, HEAD_DIM), lambda j, i: (i, 0)),
            pl.BlockSpec((tm, HEAD_DIM), lambda j, i: (i, 0)),
        ],
        out_specs=pl.BlockSpec((tm, tn), lambda j, i: (i, j)),
        out_shape=jax.ShapeDtypeStruct((rows, width), BF16),
        scratch_shapes=[pltpu.VMEM((D_MODEL, tn), BF16)],
        compiler_params=_params("arbitrary", "arbitrary"),
        name="qkv_proj",
    )(xn, w_stack, gain_row, cos, sin)


def _mm_res_kernel(a_ref, w_ref, h_ref, gate_ref, o_ref, w_scr):
    _cast_weight_once(w_ref, w_scr)
    o_ref[...] = h_ref[...] + gate_ref[0, 0] * _dot(a_ref[...], w_scr[...])


def _mm_residual(a, w_stack, w_slot, h, mods, gate_k, rows, seq, n_batch):
    _, k_dim, width = w_stack.shape
    tm = ROW_TILE
    tn = COL_TILE if k_dim >= D_MODEL else 2 * COL_TILE
    seg = functools.partial(_seg_of_tile, tm=tm, seq=seq, n_batch=n_batch)
    return pl.pallas_call(
        _mm_res_kernel,
        grid=(width // tn, rows // tm),
        in_specs=[
            pl.BlockSpec((tm, k_dim), lambda j, i: (i, 0)),
            pl.BlockSpec((1, k_dim, tn), lambda j, i: (w_slot, 0, j)),
            pl.BlockSpec((tm, tn), lambda j, i: (i, j)),
            pl.BlockSpec((1, 1, 1, tn), lambda j, i: (seg(i), gate_k, 0, j)),
        ],
        out_specs=pl.BlockSpec((tm, tn), lambda j, i: (i, j)),
        out_shape=jax.ShapeDtypeStruct((rows, width), F32),
        scratch_shapes=[pltpu.VMEM((k_dim, tn), BF16)],
        compiler_params=_params("arbitrary", "arbitrary"),
        name="mm_residual",
    )(a, w_stack, h, mods)


def _attn_kernel(*refs, n_local, group, block, kind, has_sink, has_prev, n_blocks, heads):
    refs = list(refs)
    sink_ref = refs.pop(0) if has_sink else None
    q_ref = refs.pop(0)
    k_loc = [refs.pop(0) for _ in range(n_local)]
    v_loc = [refs.pop(0) for _ in range(n_local)]
    kc_ref, vc_ref = refs.pop(0), refs.pop(0)
    bias_ref = refs.pop(0) if kind == "nat" else None
    if has_prev:
        refs.pop(0)
    o_ref = refs.pop(0)
    head0 = pl.program_id(1) * heads
    i = pl.program_id(2)
    scale = HEAD_DIM ** -0.5
    n_rows = group * block

    valid = None
    if n_local:
        row = lax.broadcasted_iota(I32, (n_rows, 1), 0)
        col = lax.broadcasted_iota(I32, (1, n_local * block), 1)
        if kind == "swa":
            rel = col - block - (row % block)
            lo = jnp.where(i == 0, block, 0)
            hi = jnp.where(i == n_blocks - 1, 2 * block, 3 * block)
            valid = (rel >= -WINDOW) & (rel <= WINDOW) & (col >= lo) & (col < hi)
        else:
            q_row = NAT_ROWS_PER_BLOCK * i + row // GRID_W
            k_row = NAT_ROWS_PER_BLOCK * (i - 1) + col // GRID_W
            r0 = jnp.clip(q_row - NAT_KH // 2, 0, n_blocks * NAT_ROWS_PER_BLOCK - NAT_KH)
            valid = (k_row >= r0) & (k_row < r0 + NAT_KH)
    g_of_row = lax.broadcasted_iota(I32, (n_rows, 1), 0) // block if has_sink else None

    outs = []
    for hd in range(heads):
        kv_cols = slice(hd * HEAD_DIM, (hd + 1) * HEAD_DIM)
        if group > 1:
            q = jnp.concatenate([q_ref[:, (hd * group + g) * HEAD_DIM:(hd * group + g + 1) * HEAD_DIM]
                                 for g in range(group)], axis=0)
        else:
            q = q_ref[:, kv_cols]
        s_ctx = _dot_nt(q, kc_ref[:, kv_cols]) * scale
        m = jnp.max(s_ctx, axis=-1, keepdims=True)
        if n_local:
            k = jnp.concatenate([r[:, kv_cols] for r in k_loc], axis=0)
            v = jnp.concatenate([r[:, kv_cols] for r in v_loc], axis=0)
            s_loc = _dot_nt(q, k) * scale
            if kind == "nat":
                s_loc = s_loc + bias_ref[hd]
            s_loc = jnp.where(valid, s_loc, MASKED)
            m = jnp.maximum(m, jnp.max(s_loc, axis=-1, keepdims=True))
        if has_sink:
            sink = jnp.zeros((n_rows, 1), F32)
            for g in range(group):
                sink = jnp.where(g_of_row == g, sink_ref[(head0 + hd) * group + g], sink)
            m = jnp.maximum(m, sink)
        p_ctx = jnp.exp(s_ctx - m)
        denom = jnp.sum(p_ctx, axis=-1, keepdims=True)
        o = _dot(p_ctx.astype(BF16), vc_ref[:, kv_cols])
        if n_local:
            p_loc = jnp.exp(s_loc - m)
            denom = denom + jnp.sum(p_loc, axis=-1, keepdims=True)
            o = o + _dot(p_loc.astype(BF16), v)
        if has_sink:
            denom = denom + jnp.exp(sink - m)
        o = (o * (1.0 / denom)).astype(BF16)
        outs += [o[g * block:(g + 1) * block] for g in range(group)]
    o_ref[...] = jnp.concatenate(outs, axis=1) if len(outs) > 1 else outs[0]


def _attention(qkv, *, kind, latent, n_batch, seq, ctx_len, k_col0, v_col0, group, out_rows,
               sink=None, bias=None, prev=None):
    heads = ATTN_HEADS_PER_STEP
    n_kv = v_col0 - k_col0
    assert n_kv % heads == 0 and k_col0 % heads == 0 and v_col0 % heads == 0
    n_steps_h = n_kv // heads
    kc0, vc0 = k_col0 // heads, v_col0 // heads
    block = (WINDOW if kind == "swa" else NAT_BLOCK) if latent else ctx_len
    n_blocks = seq // block if latent else 1
    n_local = 3 if latent else 0
    ctx_blk0 = n_batch * seq // ctx_len
    q_blk0 = 0 if latent else n_batch * seq // block
    has_sink = sink is not None
    nat_order = kind == "nat"

    def ids(a, b, c):
        return (b, a, c) if nat_order else (a, b, c)

    def q_map(a, b, c):
        bb, hh, ii = ids(a, b, c)
        return (q_blk0 + bb * n_blocks + ii, hh)

    def loc_map(delta, col0):
        def f(a, b, c):
            bb, hh, ii = ids(a, b, c)
            return (bb * n_blocks + jnp.clip(ii + delta, 0, n_blocks - 1), col0 + hh)
        return f

    def ctx_map(col0):
        def f(a, b, c):
            bb, hh, _ = ids(a, b, c)
            return (ctx_blk0 + bb, col0 + hh)
        return f

    in_specs, args = [], []
    if has_sink:
        in_specs.append(pl.BlockSpec(memory_space=pltpu.SMEM))
        args.append(sink.astype(F32))
    in_specs.append(pl.BlockSpec((block, heads * group * HEAD_DIM), q_map))
    args.append(qkv)
    for col0 in ((kc0, vc0) if latent else ()):
        for delta in (-1, 0, 1):
            in_specs.append(pl.BlockSpec((block, heads * HEAD_DIM), loc_map(delta, col0)))
            args.append(qkv)
    for col0 in (kc0, vc0):
        in_specs.append(pl.BlockSpec((ctx_len, heads * HEAD_DIM), ctx_map(col0)))
        args.append(qkv)
    use_bias = nat_order and latent
    if use_bias:
        in_specs.append(pl.BlockSpec((heads, block, 3 * block), lambda a, b, c: (a, 0, 0)))
        args.append(bias)
    aliases = {}
    if prev is not None:
        aliases = {len(args): 0}
        in_specs.append(pl.BlockSpec(memory_space=pl.ANY))
        args.append(prev)
    grid = (n_steps_h, n_batch, n_blocks) if nat_order else (n_batch, n_steps_h, n_blocks)
    kern = functools.partial(_attn_kernel, n_local=n_local, group=group, block=block,
                             kind=("nat" if use_bias else kind if latent else "ctx"),
                             has_sink=has_sink, has_prev=prev is not None, n_blocks=n_blocks, heads=heads)
    return pl.pallas_call(
        kern, grid=grid, in_specs=in_specs,
        out_specs=pl.BlockSpec((block, heads * group * HEAD_DIM), q_map),
        out_shape=jax.ShapeDtypeStruct((out_rows, n_kv * group * HEAD_DIM), BF16),
        input_output_aliases=aliases,
        compiler_params=_params("arbitrary", "arbitrary", "arbitrary"),
        name=f"attn_{kind}_{'latent' if latent else 'ctx'}",
    )(*args)


def _nat_bias_table(rpb):
    q_c = np.arange(GRID_W)[:, None]
    k_c = np.arange(GRID_W)[None, :]
    d_col = np.clip(k_c - q_c + NAT_KW - 1, 0, 2 * NAT_KW - 2)
    c0 = np.clip(q_c - NAT_KW // 2, 0, GRID_W - NAT_KW)
    col_valid = (k_c >= c0) & (k_c < c0 + NAT_KW)
    onehot = (np.arange(2 * NAT_KW - 1)[:, None, None] == d_col[None]).astype(np.float32)
    by_col = jnp.einsum("hrd,dqk->hrqk", rpb.astype(F32), jnp.asarray(onehot), precision=lax.Precision.HIGHEST)
    by_col = jnp.where(col_valid[None, None], by_col, MASKED)
    n_key_rows = 3 * NAT_ROWS_PER_BLOCK
    rows = []
    for q_r in range(NAT_ROWS_PER_BLOCK):
        d_row = [k_r - NAT_ROWS_PER_BLOCK - q_r + NAT_KH - 1 for k_r in range(n_key_rows)]
        rows.append(jnp.concatenate([by_col[:, d] for d in d_row], axis=-1))
    return jnp.concatenate(rows, axis=1)


def _dft_chan_kernel(a_ref, w_ref, o_ref):
    o_ref[0] = _dot(a_ref[...], w_ref[0]).astype(BF16)


def _dft_pos_kernel(cn_ref, sn_ref, xc_ref, xs_ref, *rest, scale):
    o_ref = rest[-1]
    o_ref[...] = ((_dot(cn_ref[...], xc_ref[0]) - _dot(sn_ref[...], xs_ref[0])) * scale).astype(BF16)


def _dft_tables(n):
    k = jnp.arange(n, dtype=I32)
    ang = ((k[:, None] * k[None, :]) % n).astype(F32) * (2.0 * np.pi / n)
    return jnp.cos(ang).astype(BF16), jnp.sin(ang).astype(BF16)


def _fourier_mix(xn, rows, n_batch, seq, ctx_len, ctx_out):
    gd = FOURIER_GROUP_DIM
    cd, sd = _dft_tables(gd)
    chan = jnp.stack([cd, sd])
    tm = ROW_TILE
    xcs = pl.pallas_call(
        _dft_chan_kernel,
        grid=(rows // tm, FOURIER_GROUPS, 2),
        in_specs=[pl.BlockSpec((tm, gd), lambda i, g, t: (i, g)),
                  pl.BlockSpec((1, gd, gd), lambda i, g, t: (t, 0, 0))],
        out_specs=pl.BlockSpec((1, tm, gd), lambda i, g, t: (t, i, g)),
        out_shape=jax.ShapeDtypeStruct((2, rows, D_MODEL), BF16),
        compiler_params=_params("arbitrary", "arbitrary", "arbitrary"),
        name="dft_channels",
    )(xn, chan)

    def pos_call(n, tmo, row_blk0, in_blk0, out_rows, prev):
        cn, sn = _dft_tables(n)
        tn = COL_TILE
        per = n // tmo
        in_specs = [
            pl.BlockSpec((tmo, n), lambda b, i, j: (i, 0)),
            pl.BlockSpec((tmo, n), lambda b, i, j: (i, 0)),
            pl.BlockSpec((1, n, tn), lambda b, i, j: (0, in_blk0 + b, j)),
            pl.BlockSpec((1, n, tn), lambda b, i, j: (1, in_blk0 + b, j)),
        ]
        args = [cn, sn, xcs, xcs]
        aliases = {}
        if prev is not None:
            in_specs.append(pl.BlockSpec(memory_space=pl.ANY))
            args.append(prev)
            aliases = {4: 0}
        return pl.pallas_call(
            functools.partial(_dft_pos_kernel, scale=float((n * gd) ** -0.5)),
            grid=(n_batch, per, D_MODEL // tn),
            in_specs=in_specs,
            out_specs=pl.BlockSpec((tmo, tn), lambda b, i, j: (row_blk0 + b * per + i, j)),
            out_shape=jax.ShapeDtypeStruct((out_rows, D_MODEL), BF16),
            input_output_aliases=aliases,
            compiler_params=_params("arbitrary", "arbitrary", "arbitrary"),
            name=f"dft_positions_{n}",
        )(*args)

    z = pos_call(seq, 512, 0, 0, rows, None)
    if ctx_out:
        z = pos_call(ctx_len, ctx_len, n_batch * seq // ctx_len, n_batch * seq // ctx_len, rows, z)
    return z


def _slab_copy(src_ref, src_token, dst_ref, dst_token, sem):
    src = src_ref.at[pl.ds(pl.multiple_of(src_token * SLAB, SLAB), SLAB)]
    dst = dst_ref.at[pl.ds(pl.multiple_of(dst_token * SLAB, SLAB), SLAB)]
    return pltpu.make_async_copy(src, dst, sem)


def _load_slots(pos_ref, pos_smem, sem, tile=0):
    cp = pltpu.make_async_copy(pos_ref.at[tile, 0], pos_smem, sem)
    cp.start()
    cp.wait()


def _for_token_groups(n_tokens, fn):
    def group(g, c):
        for k in range(TOP_K):
            for u in range(DMA_UNROLL):
                fn(k, g * DMA_UNROLL + u)
        return c
    lax.fori_loop(0, n_tokens // DMA_UNROLL, group, 0)


def _dispatch_step(i, n_steps, pos_tile, cnt_ref, off_ref, pos_ref, xw_ref, xs_ref,
                   pos_smem, zero_ref, xbuf_ref, sem_idx, sem_load, sem_rows):
    tm = pos_smem.shape[0] // TOP_K

    @pl.when(i == 0)
    def _():
        zero_ref[...] = jnp.zeros_like(zero_ref)

        def per_expert(e, carry):
            count = cnt_ref[e]
            first = off_ref[e] + count
            n_pad = (count + EXPERT_TILE - 1) // EXPERT_TILE * EXPERT_TILE - count

            def start(r, c):
                _slab_copy(zero_ref, 0, xs_ref, first + r, sem_rows.at[0]).start()
                return c

            def wait(r, c):
                _slab_copy(zero_ref, 0, xs_ref, first + r, sem_rows.at[0]).wait()
                return c

            lax.fori_loop(0, n_pad, start, 0)
            lax.fori_loop(0, n_pad, wait, 0)
            return carry

        lax.fori_loop(0, N_EXPERTS, per_expert, 0)

    def load(step):
        which = step % N_DISPATCH_BUFS
        src = xw_ref.at[pl.ds(pl.multiple_of(step * tm * SLAB, tm * SLAB), tm * SLAB)]
        return pltpu.make_async_copy(src, xbuf_ref.at[which], sem_load.at[which])

    def drain(step):
        which = step % N_DISPATCH_BUFS
        _for_token_groups(tm, lambda k, t: _slab_copy(xbuf_ref.at[which], t, xs_ref, 0, sem_rows.at[which]).wait())

    @pl.when(i == 0)
    def _():
        load(i).start()

    @pl.when(i >= N_DISPATCH_BUFS - 1)
    def _():
        drain(i - (N_DISPATCH_BUFS - 1))

    @pl.when(i + 1 < n_steps)
    def _():
        load(i + 1).start()

    load(i).wait()
    _load_slots(pos_ref, pos_smem, sem_idx, pos_tile)
    cur = i % N_DISPATCH_BUFS
    _for_token_groups(tm, lambda k, t: _slab_copy(
        xbuf_ref.at[cur], t, xs_ref, pos_smem[k * tm + t], sem_rows.at[cur]).start())

    @pl.when(i == n_steps - 1)
    def _():
        for back in reversed(range(N_DISPATCH_BUFS - 1)):
            @pl.when(i - back >= 0)
            def _(back=back):
                drain(i - back)


def _swiglu_dispatch_kernel(cnt_ref, off_ref, a_ref, wg_ref, wu_ref, pos_ref, xw_ref, o_ref, xs_ref,
                            wg_scr, wu_scr, *dispatch_scratch):
    n_inner = pl.num_programs(1)
    step = pl.program_id(0) * n_inner + pl.program_id(1)
    per_step = pos_ref.shape[0]
    for t in range(per_step):
        _dispatch_step(step * per_step + t, pl.num_programs(0) * n_inner * per_step, t,
                       cnt_ref, off_ref, pos_ref, xw_ref, xs_ref, *dispatch_scratch)
    _cast_weight_once(wg_ref, wg_scr)
    _cast_weight_once(wu_ref, wu_scr)
    a = a_ref[...]
    o_ref[...] = (_silu(_dot(a, wg_scr[...])) * _dot(a, wu_scr[...])).astype(BF16)


def _swiglu_and_dispatch(a, wg_stack, wu_stack, layer, rows, xw, pos_tiles, counts, offs, n_slots):
    width = wg_stack.shape[2]
    tm, tn, tok = ROW_TILE // 2, COL_TILE, TOKEN_TILE
    n_inner = rows // tm
    per_step = (rows // tok) // ((width // tn) * n_inner)
    assert per_step * (width // tn) * n_inner == rows // tok
    w_spec = pl.BlockSpec((1, D_MODEL, tn), lambda j, i, cnt, off: (layer, 0, j), pipeline_mode=pl.Buffered(1))
    grid_spec = pltpu.PrefetchScalarGridSpec(
        num_scalar_prefetch=2,
        grid=(width // tn, n_inner),
        in_specs=[
            pl.BlockSpec((tm, D_MODEL), lambda j, i, cnt, off: (i, 0)),
            w_spec, w_spec,
            pl.BlockSpec((per_step, 1, TOP_K * tok), lambda j, i, cnt, off: (j * n_inner + i, 0, 0)),
            pl.BlockSpec(memory_space=pl.ANY),
        ],
        out_specs=[pl.BlockSpec((tm, tn), lambda j, i, cnt, off: (i, j)), pl.BlockSpec(memory_space=pl.ANY)],
        scratch_shapes=[
            pltpu.VMEM((D_MODEL, tn), BF16),
            pltpu.VMEM((D_MODEL, tn), BF16),
            pltpu.SMEM((TOP_K * tok,), I32),
            pltpu.VMEM((SLAB, LANES), U32),
            pltpu.VMEM((N_DISPATCH_BUFS, tok * SLAB, LANES), U32),
            pltpu.SemaphoreType.DMA,
            pltpu.SemaphoreType.DMA((N_DISPATCH_BUFS,)),
            pltpu.SemaphoreType.DMA((N_DISPATCH_BUFS,)),
        ],
    )
    return pl.pallas_call(
        _swiglu_dispatch_kernel, grid_spec=grid_spec,
        out_shape=[jax.ShapeDtypeStruct((rows, width), BF16), jax.ShapeDtypeStruct((n_slots * SLAB, LANES), U32)],
        compiler_params=_params("arbitrary", "arbitrary", disable_bounds_checks=True),
        name="shared_swiglu_dispatch",
    )(counts, offs, a, wg_stack, wu_stack, pos_tiles, xw)


def _expert_kernel(te_ref, nu_ref, x_ref, wg_ref, wu_ref, wd_ref, o_ref, wgu_scr, wd_scr):
    j = pl.program_id(0)

    @pl.when(j < nu_ref[0])
    def _():
        @pl.when((j == 0) | (te_ref[j] != te_ref[jnp.maximum(j - 1, 0)]))
        def _():
            wgu_scr[:, :EXPERT_HIDDEN] = wg_ref[0, 0].astype(BF16)
            wgu_scr[:, EXPERT_HIDDEN:] = wu_ref[0, 0].astype(BF16)
            wd_scr[...] = wd_ref[0, 0].astype(BF16)

        x_lo, x_hi = _unpack_halves(_load_slabs(x_ref, EXPERT_TILE))
        x_lo, x_hi = x_lo.astype(BF16), x_hi.astype(BF16)
        gate_up = _dot(x_lo, wgu_scr[:HALF_D]) + _dot(x_hi, wgu_scr[HALF_D:])
        hid = (_silu(gate_up[:, :EXPERT_HIDDEN]) * gate_up[:, EXPERT_HIDDEN:]).astype(BF16)
        _store_slabs(o_ref, _pack_halves(_dot(hid, wd_scr[:, :HALF_D]), _dot(hid, wd_scr[:, HALF_D:])))


def _experts(x_sorted, tile_expert, n_used, wg, wu, wd, layer):
    n_rows = x_sorted.shape[0] // SLAB
    tm = EXPERT_TILE

    def row_map(j, te, nu):
        return (jnp.minimum(j, nu[0] - 1), 0)

    def w_map(j, te, nu):
        return (layer, te[j], 0, 0)

    grid_spec = pltpu.PrefetchScalarGridSpec(
        num_scalar_prefetch=2,
        grid=(n_rows // tm,),
        in_specs=[
            pl.BlockSpec((tm * SLAB, LANES), row_map),
            pl.BlockSpec((1, 1, D_MODEL, EXPERT_HIDDEN), w_map),
            pl.BlockSpec((1, 1, D_MODEL, EXPERT_HIDDEN), w_map),
            pl.BlockSpec((1, 1, EXPERT_HIDDEN, D_MODEL), w_map),
        ],
        out_specs=pl.BlockSpec((tm * SLAB, LANES), row_map),
        scratch_shapes=[
            pltpu.VMEM((D_MODEL, 2 * EXPERT_HIDDEN), BF16),
            pltpu.VMEM((EXPERT_HIDDEN, D_MODEL), BF16),
        ],
    )
    return pl.pallas_call(
        _expert_kernel, grid_spec=grid_spec,
        out_shape=jax.ShapeDtypeStruct((n_rows * SLAB, LANES), U32),
        compiler_params=_params("arbitrary"),
        name="routed_experts",
    )(tile_expert, n_used, x_sorted, wg, wu, wd)


def _combine_kernel(pos_ref, nxt_ref, h_ref, w_ref, gate_ref, y_ref, o_ref, pos_smem, buf_ref, sem_idx, sem_rows):
    i = pl.program_id(0)
    n_steps = pl.num_programs(0)
    tm = h_ref.shape[0]
    cur = i % 2

    def gather(slot_ref, which):
        _load_slots(slot_ref, pos_smem, sem_idx)
        _for_token_groups(tm, lambda k, t: _slab_copy(
            y_ref, pos_smem[k * tm + t], buf_ref.at[which, k], t, sem_rows.at[which]).start())

    @pl.when(i == 0)
    def _():
        gather(pos_ref, 0)

    @pl.when(i + 1 < n_steps)
    def _():
        gather(nxt_ref, 1 - cur)

    _for_token_groups(tm, lambda k, t: _slab_copy(y_ref, 0, buf_ref.at[cur, k], t, sem_rows.at[cur]).wait())
    sub = 8

    def rows_chunk(r, c):
        first = pl.multiple_of(r * sub, sub)
        rs = pl.ds(first, sub)
        w = w_ref[rs, :]
        wk = [jnp.broadcast_to(w[:, k:k + 1], (sub, LANES)) for k in range(TOP_K)]
        for part in range(SLAB):
            terms = []
            for k in range(TOP_K):
                y_lo, y_hi = _unpack_halves(buf_ref[cur, k, _slab_rows(first, sub, part), :])
                terms.append((wk[k] * y_lo, wk[k] * y_hi))
            acc_lo = (terms[0][0] + terms[1][0]) + (terms[2][0] + terms[3][0])
            acc_hi = (terms[0][1] + terms[1][1]) + (terms[2][1] + terms[3][1])
            lo = pl.ds(part * LANES, LANES)
            hi = pl.ds(HALF_D + part * LANES, LANES)
            o_ref[rs, lo] = h_ref[rs, lo] + gate_ref[0, 0, :, lo] * acc_lo
            o_ref[rs, hi] = h_ref[rs, hi] + gate_ref[0, 0, :, hi] * acc_hi
        return c

    lax.fori_loop(0, tm // sub, rows_chunk, 0)


def _combine(h, y_sorted, pos_tiles, wts_t, mods, gate_k, rows, seq, n_batch):
    tm = TOKEN_TILE
    n_steps = rows // tm
    seg = functools.partial(_seg_of_tile, tm=tm, seq=seq, n_batch=n_batch)
    return pl.pallas_call(
        _combine_kernel,
        grid=(n_steps,),
        in_specs=[
            pl.BlockSpec((1, 1, TOP_K * tm), lambda i: (i, 0, 0)),
            pl.BlockSpec((1, 1, TOP_K * tm), lambda i: (jnp.minimum(i + 1, n_steps - 1), 0, 0)),
            pl.BlockSpec((tm, D_MODEL), lambda i: (i, 0)),
            pl.BlockSpec((tm, TOP_K), lambda i: (i, 0)),
            pl.BlockSpec((1, 1, 1, D_MODEL), lambda i: (seg(i), gate_k, 0, 0)),
            pl.BlockSpec(memory_space=pl.ANY),
        ],
        out_specs=pl.BlockSpec((tm, D_MODEL), lambda i: (i, 0)),
        out_shape=jax.ShapeDtypeStruct((rows, D_MODEL), F32),
        scratch_shapes=[
            pltpu.SMEM((TOP_K * tm,), I32),
            pltpu.VMEM((2, TOP_K, tm * SLAB, LANES), U32),
            pltpu.SemaphoreType.DMA,
            pltpu.SemaphoreType.DMA((2,)),
        ],
        compiler_params=_params("arbitrary", disable_bounds_checks=True),
        name="moe_combine",
    )(pos_tiles, pos_tiles, h, wts_t, mods, y_sorted)


def _dispatch_plan(idx, rank, counts_f, rows):
    tm = EXPERT_TILE
    counts = counts_f[:, 0].astype(I32)
    padded = (counts + tm - 1) // tm * tm
    ends = jnp.cumsum(padded)
    offs = ends - padded
    expert = jnp.arange(N_EXPERTS, dtype=I32)[:, None, None]
    pos = rank + jnp.sum(jnp.where(idx[None] == expert, offs[:, None, None], 0), axis=0)
    n_tok_tiles = rows // TOKEN_TILE
    pos_tiles = pos.reshape(TOP_K, n_tok_tiles, TOKEN_TILE).transpose(1, 0, 2).reshape(n_tok_tiles, 1, TOP_K * TOKEN_TILE)
    n_slots = TOP_K * rows + N_EXPERTS * tm
    n_used = ends[-1] // tm
    tile_start = jnp.arange(n_slots // tm, dtype=I32) * tm
    tile_expert = jnp.sum((tile_start[:, None] >= ends[None, :]).astype(I32), axis=1)
    last_expert = jnp.sum((((n_used - 1) * tm) >= ends).astype(I32))
    tile_expert = jnp.where(tile_start < ends[-1], tile_expert, last_expert).astype(I32)
    return pos_tiles, counts, offs.astype(I32), tile_expert, n_used.astype(I32).reshape(1), n_slots


def _moe(h, g2, mods, layer, rows, seq, n_batch, w_router, e_bias, wg, wu, wd, sg, su, sd):
    xn, xw, idx, wts, rank, counts_f = _norm_mod(h, g2, mods, 3, 4, rows, seq, n_batch, router=(w_router, e_bias))
    pos_tiles, counts, offs, tile_expert, n_used, n_slots = _dispatch_plan(idx, rank, counts_f, rows)
    hid_s, x_sorted = _swiglu_and_dispatch(xn, sg, su, layer, rows, xw, pos_tiles, counts, offs, n_slots)
    h = _mm_residual(hid_s, sd, layer, h, mods, 5, rows, seq, n_batch)
    y_sorted = _experts(x_sorted, tile_expert, n_used, wg, wu, wd, layer)
    return _combine(h, y_sorted, pos_tiles, wts.T, mods, 5, rows, seq, n_batch)


def _rope_tables(n_batch, seq, total_rows):
    pos = jnp.arange(seq)
    row = (pos // GRID_W).astype(F32)
    col = (pos % GRID_W).astype(F32)
    inv = ROPE_THETA ** (-jnp.arange(0, ROPE_AXIS_DIM, 2, dtype=F32) / ROPE_AXIS_DIM)
    ar = row[:, None] * inv
    ac = col[:, None] * inv
    cos = jnp.concatenate([jnp.cos(ar), jnp.cos(ar), jnp.cos(ac), jnp.cos(ac)], axis=-1)
    sin = jnp.concatenate([-jnp.sin(ar), jnp.sin(ar), -jnp.sin(ac), jnp.sin(ac)], axis=-1)
    n_ctx_rows = total_rows - n_batch * seq
    cos = jnp.concatenate([jnp.tile(cos, (n_batch, 1)), jnp.ones((n_ctx_rows, HEAD_DIM), F32)], axis=0)
    sin = jnp.concatenate([jnp.tile(sin, (n_batch, 1)), jnp.zeros((n_ctx_rows, HEAD_DIM), F32)], axis=0)
    return cos, sin


def kernel(x, c, ctx, c_ctx, ada_down, ada_up, ada_bias, norm1_g, norm2_g, swa_w_qkv, swa_w_o, swa_q_gain, swa_k_gain, swa_sink, fnet_w_o, nat_w_qkv, nat_w_o, nat_q_gain, nat_k_gain, nat_rpb, moe_router, moe_bias, moe_w_gate, moe_w_up, moe_w_down, shared_w_gate, shared_w_up, shared_w_down):
    n_batch, seq, _ = x.shape
    ctx_len = ctx.shape[1]
    lat_rows = n_batch * seq
    all_rows = lat_rows + n_batch * ctx_len
    assert n_batch + 1 <= COND_ROWS and seq % ROW_TILE == 0 and lat_rows % ROW_TILE == 0 and all_rows % ROW_TILE == 0
    assert seq // GRID_W >= NAT_KH and seq % NAT_BLOCK == 0 and ctx_len == NAT_BLOCK

    h = jnp.concatenate([x.reshape(lat_rows, D_MODEL), ctx.reshape(n_batch * ctx_len, D_MODEL)], axis=0)
    cond = jnp.concatenate([c, c_ctx[None, :], jnp.zeros((COND_ROWS - n_batch - 1, D_MODEL), F32)], axis=0)
    mods_all = _ada_mods(cond, ada_down, ada_up, ada_bias).reshape(DEPTH, COND_ROWS, N_MOD, 1, D_MODEL)
    cos, sin = _rope_tables(n_batch, seq, all_rows)
    ones_row = jnp.ones((N_HEADS * HEAD_DIM,), F32)

    slot = [0, 0, 0]
    for layer in range(DEPTH):
        kind = layer % 3
        s = slot[kind]
        slot[kind] += 1
        ctx_out = layer < DEPTH - 1
        rows = all_rows if ctx_out else lat_rows
        mods = mods_all[layer]
        common = dict(n_batch=n_batch, seq=seq, ctx_len=ctx_len, out_rows=rows)
        xn = _norm_mod(h, norm1_g[layer], mods, 0, 1, all_rows, seq, n_batch)
        if kind == 0:
            nq, nkv = N_HEADS * HEAD_DIM, N_KV_HEADS * HEAD_DIM
            gain = jnp.concatenate([jnp.tile(swa_q_gain[s], N_HEADS), jnp.tile(swa_k_gain[s], N_KV_HEADS),
                                    ones_row[:nkv]])[None, :]
            qkv = _qkv_proj(xn, swa_w_qkv, s, gain, cos, sin, nq + nkv, rope=True)
            att = dict(kind="swa", k_col0=N_HEADS, v_col0=N_HEADS + N_KV_HEADS, group=GQA_GROUP,
                       sink=swa_sink[s], **common)
            o = _attention(qkv, latent=True, **att)
            if ctx_out:
                o = _attention(qkv, latent=False, prev=o, **att)
            w_o = swa_w_o
        elif kind == 1:
            o = _fourier_mix(xn, all_rows, n_batch, seq, ctx_len, ctx_out)
            w_o = fnet_w_o
        else:
            width = N_HEADS * HEAD_DIM
            gain = jnp.concatenate([jnp.tile(nat_q_gain[s], N_HEADS), jnp.tile(nat_k_gain[s], N_HEADS),
                                    ones_row])[None, :]
            qkv = _qkv_proj(xn, nat_w_qkv, s, gain, cos, sin, 2 * width, rope=False)
            att = dict(kind="nat", k_col0=N_HEADS, v_col0=2 * N_HEADS, group=1, **common)
            o = _attention(qkv, latent=True, bias=_nat_bias_table(nat_rpb[s]), **att)
            if ctx_out:
                o = _attention(qkv, latent=False, prev=o, **att)
            w_o = nat_w_o
        h = _mm_residual(o, w_o, s, h, mods, 2, rows, seq, n_batch)
        h = _moe(h, norm2_g[layer], mods, layer, rows, seq, n_batch, moe_router[layer], moe_bias[layer],
                 moe_w_gate, moe_w_up, moe_w_down, shared_w_gate, shared_w_up, shared_w_down)
    return h[:lat_rows].reshape(n_batch, seq, D_MODEL)
```
